```python
import jax, jax.numpy as jnp
from jax import lax
import numpy as np

D_MODEL = 1024
BATCH = 16
SEQ = 2048
DEPTH = 4

N_MIXERS = 2
N_LAYERS_A = (DEPTH + N_MIXERS - 1) // N_MIXERS
N_LAYERS_B = DEPTH // N_MIXERS
CHUNK = 128
SGU_HEADS = 8
SGU_WIDTH = 2 * D_MODEL
SGU_HEAD_DIM = SGU_WIDTH // SGU_HEADS
POOL_WINDOWS = (2, 4, 8, 16)
POOL_GROUPS = len(POOL_WINDOWS)
POOL_WIDTH = D_MODEL
POOL_GROUP_DIM = POOL_WIDTH // POOL_GROUPS
N_EXPERT_GROUPS = 4
EXPERTS_PER_GROUP = 8
N_EXPERTS = N_EXPERT_GROUPS * EXPERTS_PER_GROUP
TOP_K = 2
D_EXPERT = D_MODEL // 2
EXPERT_BLOCK = 128
EPS = 1e-6

kernel_name = "hybrid_sgu_pool_hmoe_encoder"


def rmsnorm(x, g):
    xf = x.astype(jnp.float32)
    y = xf * lax.rsqrt(jnp.mean(xf * xf, axis=-1, keepdims=True) + EPS)
    return (y * g.astype(jnp.float32)).astype(x.dtype)


def sgu_mixer(h, w_in, b_in, g_v, w_s, b_s, w_out):
    bsz, seq, _ = h.shape
    z = jax.nn.gelu(h @ w_in + b_in, approximate=False)
    u, v = jnp.split(z, 2, axis=-1)
    v = rmsnorm(v, g_v)
    v = v.reshape(bsz, seq // CHUNK, CHUNK, SGU_HEADS, SGU_HEAD_DIM)
    sv = jnp.einsum('hts,bcshd->bcthd', w_s, v) + b_s.T[:, :, None]
    y = u * sv.reshape(bsz, seq, SGU_WIDTH)
    return y @ w_out


def pool_mixer(h, w_in, w_group, scale, w_out):
    bsz, seq, _ = h.shape
    v = h @ w_in
    csum = jnp.cumsum(jnp.pad(v.astype(jnp.float32), ((0, 0), (1, 0), (0, 0))), axis=1)
    pos = jnp.arange(seq)
    outs = []
    for g, w in enumerate(POOL_WINDOWS):
        sl = slice(g * POOL_GROUP_DIM, (g + 1) * POOL_GROUP_DIM)
        lo = jnp.clip(pos - w // 2, 0, seq)
        hi = jnp.clip(pos - w // 2 + w, 0, seq)
        cg = csum[:, :, sl]
        cnt = (hi - lo).astype(jnp.float32)[None, :, None]
        mean = (jnp.take(cg, hi, axis=1) - jnp.take(cg, lo, axis=1)) / cnt
        d = mean.astype(v.dtype) - v[..., sl]
        outs.append(d @ w_group[g])
    y = jnp.concatenate(outs, axis=-1) * scale
    return y @ w_out


def hierarchical_moe(h, rg_w, rg_b, re_w, re_b, w_gate, w_up, w_down):
    bsz, seq, d = h.shape
    n_tok = bsz * seq
    xt = h.reshape(n_tok, d)
    xf = xt.astype(jnp.float32)
    g_logits = xf @ rg_w.astype(jnp.float32) + rg_b.astype(jnp.float32)
    g_prob = jax.nn.softmax(g_logits, axis=-1)
    g_w, g_idx = lax.top_k(g_prob, 1)
    e_logits = (xf @ re_w.astype(jnp.float32) + re_b.astype(jnp.float32)).reshape(
        n_tok, N_EXPERT_GROUPS, EXPERTS_PER_GROUP)
    e_sel = jnp.take_along_axis(e_logits, g_idx[:, :, None], axis=1)[:, 0]
    e_prob = jax.nn.softmax(e_sel, axis=-1)
    e_w, e_idx = lax.top_k(e_prob, TOP_K)
    e_w = e_w / jnp.sum(e_w, axis=-1, keepdims=True)
    gate = g_w * e_w
    expert_id = g_idx * EXPERTS_PER_GROUP + e_idx

    n_assign = n_tok * TOP_K
    flat_e = expert_id.reshape(-1)
    order = jnp.argsort(flat_e)
    sorted_e = flat_e[order]
    tok = order // TOP_K
    counts = jnp.bincount(flat_e, length=N_EXPERTS)
    starts = jnp.cumsum(counts) - counts
    padded = ((counts + EXPERT_BLOCK - 1) // EXPERT_BLOCK) * EXPERT_BLOCK
    pend = jnp.cumsum(padded)
    pstart = pend - padded
    dest = pstart[sorted_e] + (jnp.arange(n_assign) - starts[sorted_e])
    n_blocks = -(-n_assign // EXPERT_BLOCK) + N_EXPERTS
    buf = jnp.zeros((n_blocks * EXPERT_BLOCK, d), xt.dtype).at[dest].set(xt[tok])
    block_expert = jnp.minimum(
        jnp.searchsorted(pend, jnp.arange(n_blocks) * EXPERT_BLOCK, side='right'), N_EXPERTS - 1)

    def expert_block(args):
        xb, e = args
        hid = jax.nn.silu(xb @ w_gate[e]) * (xb @ w_up[e])
        return hid @ w_down[e]

    yb = lax.map(expert_block, (buf.reshape(n_blocks, EXPERT_BLOCK, d), block_expert))
    y_sorted = yb.reshape(n_blocks * EXPERT_BLOCK, d)[dest]
    w_sorted = gate.reshape(-1)[order].astype(y_sorted.dtype)
    out = jax.ops.segment_sum(y_sorted * w_sorted[:, None], tok, num_segments=n_tok)
    return out.reshape(bsz, seq, d)


def setup_inputs(seed: int = 0) -> dict:
    key = jax.random.key(seed)
    ks = jax.random.split(key, 21)

    def nrm(k, shape, scale):
        return jax.random.normal(k, shape, jnp.float32) * scale

    D, F, E = D_MODEL, D_EXPERT, N_EXPERTS
    return {
        "x": nrm(ks[0], (BATCH, SEQ, D), 1.0),
        "norm_mix_g": 1.0 + nrm(ks[1], (DEPTH, D), 0.02),
        "norm_ffn_g": 1.0 + nrm(ks[2], (DEPTH, D), 0.02),
        "a_w_in": nrm(ks[3], (N_LAYERS_A, D, 2 * SGU_WIDTH), D ** -0.5),
        "a_b_in": nrm(ks[4], (N_LAYERS_A, 2 * SGU_WIDTH), 0.02),
        "a_norm_v": 1.0 + nrm(ks[5], (N_LAYERS_A, SGU_WIDTH), 0.02),
        "a_w_spatial": nrm(ks[6], (N_LAYERS_A, SGU_HEADS, CHUNK, CHUNK), CHUNK ** -0.5),
        "a_b_spatial": 1.0 + nrm(ks[7], (N_LAYERS_A, SGU_HEADS, CHUNK), 0.02),
        "a_w_out": nrm(ks[8], (N_LAYERS_A, SGU_WIDTH, D), SGU_WIDTH ** -0.5),
        "b_w_in": nrm(ks[9], (N_LAYERS_B, D, POOL_WIDTH), D ** -0.5),
        "b_w_group": nrm(ks[10], (N_LAYERS_B, POOL_GROUPS, POOL_GROUP_DIM, POOL_GROUP_DIM),
                         POOL_GROUP_DIM ** -0.5),
        "b_scale": 1.0 + nrm(ks[11], (N_LAYERS_B, POOL_WIDTH), 0.02),
        "b_w_out": nrm(ks[12], (N_LAYERS_B, POOL_WIDTH, D), POOL_WIDTH ** -0.5),
        "router_group_w": nrm(ks[13], (DEPTH, D, N_EXPERT_GROUPS), D ** -0.5),
        "router_group_b": nrm(ks[14], (DEPTH, N_EXPERT_GROUPS), 0.01),
        "router_expert_w": nrm(ks[15], (DEPTH, D, E), D ** -0.5),
        "router_expert_b": nrm(ks[16], (DEPTH, E), 0.01),
        "w_gate": nrm(ks[17], (DEPTH, E, D, F), D ** -0.5),
        "w_up": nrm(ks[18], (DEPTH, E, D, F), D ** -0.5),
        "w_down": nrm(ks[19], (DEPTH, E, F, D), F ** -0.5),
        "final_norm_g": 1.0 + nrm(ks[20], (D,), 0.02),
    }


def reference(x, norm_mix_g, norm_ffn_g, a_w_in, a_b_in, a_norm_v, a_w_spatial, a_b_spatial,
              a_w_out, b_w_in, b_w_group, b_scale, b_w_out, router_group_w, router_group_b,
              router_expert_w, router_expert_b, w_gate, w_up, w_down, final_norm_g):
    for i in range(DEPTH):
        h = rmsnorm(x, norm_mix_g[i])
        j = i // N_MIXERS
        if i % N_MIXERS == 0:
            x = x + sgu_mixer(h, a_w_in[j], a_b_in[j], a_norm_v[j], a_w_spatial[j],
                              a_b_spatial[j], a_w_out[j])
        else:
            x = x + pool_mixer(h, b_w_in[j], b_w_group[j], b_scale[j], b_w_out[j])
        h = rmsnorm(x, norm_ffn_g[i])
        x = x + hierarchical_moe(h, router_group_w[i], router_group_b[i], router_expert_w[i],
                                 router_expert_b[i], w_gate[i], w_up[i], w_down[i])
    return rmsnorm(x, final_norm_g)
```

```python
import functools

import jax
import jax.numpy as jnp
from jax import lax
from jax.experimental import pallas as pl
from jax.experimental.pallas import tpu as pltpu

F32 = jnp.float32
BF16 = jnp.bfloat16
I32 = jnp.int32

D = 1024
BATCH = 16
SEQ = 2048
T = BATCH * SEQ
DEPTH = 4
CHUNK = 128
HEADS = 8
SGU_W = 2 * D
HEAD_DIM = SGU_W // HEADS
POOL_WINDOWS = (2, 4, 8, 16)
GROUP_DIM = D // len(POOL_WINDOWS)
N_GROUPS = 4
EPG = 8
N_EXPERTS = N_GROUPS * EPG
TOP_K = 2
D_EXPERT = D // 2
EPS = 1e-6

LANES = 128
SUBLANES = 8
TM = 256
TE = 256
N_TILES = (T * TOP_K) // TE + N_EXPERTS
N_SLOTS = N_TILES * TE
TD = 512
HALO = SUBLANES
EXPERT_LANE0 = N_GROUPS
N_COL = 512
VMEM_LIMIT = 56 * 1024 * 1024


def _rms(x, g):
    return x * lax.rsqrt(jnp.mean(x * x, axis=-1, keepdims=True) + EPS) * g


def _combine(xp, y2, ri):
    return xp + (ri[:, 0:1] * y2[:, :D] + ri[:, 1:2] * y2[:, D:])


def _route(xn, gffn_ref, wr_ref, br_ref, run_ref, h2_ref, ri_ref, cnt_ref):
    tm = xn.shape[0]
    h2 = _rms(xn, gffn_ref[...])
    h2_ref[...] = h2
    logits = jnp.dot(h2.astype(BF16), wr_ref[...], preferred_element_type=F32) + br_ref[...]
    lane = lax.broadcasted_iota(I32, (tm, LANES), 1).astype(F32)
    big = float(LANES)

    def first_lane(mask):
        return jnp.min(jnp.where(mask, lane, big), axis=-1, keepdims=True)

    gmask = lane < float(N_GROUPS)
    gmax = jnp.max(jnp.where(gmask, logits, -jnp.inf), axis=-1, keepdims=True)
    gexp = jnp.where(gmask, jnp.exp(logits - gmax), 0.0)
    gprob = gexp / jnp.sum(gexp, axis=-1, keepdims=True)
    g_w = jnp.max(gprob, axis=-1, keepdims=True)
    g_idx = first_lane(gmask & (gprob == g_w))
    lo = float(EXPERT_LANE0) + float(EPG) * g_idx
    emask = (lane >= lo) & (lane < lo + float(EPG))
    emax = jnp.max(jnp.where(emask, logits, -jnp.inf), axis=-1, keepdims=True)
    eexp = jnp.where(emask, jnp.exp(logits - emax), 0.0)
    eprob = jnp.where(emask, eexp / jnp.sum(eexp, axis=-1, keepdims=True), -1.0)
    p1 = jnp.max(eprob, axis=-1, keepdims=True)
    i1 = first_lane(eprob == p1)
    eprob2 = jnp.where(lane == i1, -1.0, eprob)
    p2 = jnp.max(eprob2, axis=-1, keepdims=True)
    i2 = first_lane(eprob2 == p2)
    den = p1 + p2
    gate1 = g_w * (p1 / den)
    gate2 = g_w * (p2 / den)
    oh1 = lane == i1
    oh2 = lane == i2
    row = lax.broadcasted_iota(I32, (tm, tm), 0)
    col = lax.broadcasted_iota(I32, (tm, tm), 1)
    ltri = (col < row).astype(BF16)
    c1 = jnp.dot(ltri, oh1.astype(BF16), preferred_element_type=F32)
    c2 = jnp.dot(ltri, oh2.astype(BF16), preferred_element_type=F32)
    tot1 = jnp.sum(oh1.astype(F32), axis=0, keepdims=True)
    tot2 = jnp.sum(oh2.astype(F32), axis=0, keepdims=True)
    run = run_ref[...]
    r1 = jnp.sum(jnp.where(oh1, run + c1, 0.0), axis=-1, keepdims=True)
    r2 = jnp.sum(jnp.where(oh2, run + tot1 + c2, 0.0), axis=-1, keepdims=True)
    run = run + tot1 + tot2
    run_ref[...] = run
    cnt_ref[...] = run
    e0 = float(EXPERT_LANE0)
    vals = (gate1, gate2, i1 - e0, i2 - e0, r1, r2)
    ri = jnp.zeros((tm, LANES), F32)
    for k, v in enumerate(vals):
        ri = jnp.where(lane == float(k), v, ri)
    ri_ref[...] = ri


def _gelu_exact(z):
    return 0.5 * z * (1.0 + lax.erf(z * (0.5 ** 0.5)))


def _sgu_kernel(has_moe, *refs):
    if has_moe:
        xp_ref, y2_ref, rip_ref = refs[:3]
        refs = refs[3:]
    else:
        xp_ref = refs[0]
        refs = refs[1:]
    (gmix_ref, win_ref, bin_ref, gv_ref, ws_ref, bst_ref, wout_ref, gffn_ref, wr_ref, br_ref,
     xn_ref, h2_ref, ri_ref, cnt_ref, u_ref, v_ref, y_ref, run_ref) = refs

    @pl.when(pl.program_id(0) == 0)
    def _():
        run_ref[...] = jnp.zeros_like(run_ref)

    x = xp_ref[...]
    if has_moe:
        x = _combine(x, y2_ref[...], rip_ref[...])
    h = _rms(x, gmix_ref[...]).astype(BF16)
    ss = jnp.zeros((TM, 1), F32)
    for j in range(0, SGU_W, N_COL):
        z = jnp.dot(h, win_ref[:, j:j + N_COL], preferred_element_type=F32) + bin_ref[:, j:j + N_COL]
        u_ref[:, j:j + N_COL] = _gelu_exact(z)
    for j in range(0, SGU_W, N_COL):
        jj = SGU_W + j
        z = jnp.dot(h, win_ref[:, jj:jj + N_COL], preferred_element_type=F32) + bin_ref[:, jj:jj + N_COL]
        vv = _gelu_exact(z)
        v_ref[:, j:j + N_COL] = vv
        ss = ss + jnp.sum(vv * vv, axis=-1, keepdims=True)
    rinv = lax.rsqrt(ss * (1.0 / SGU_W) + EPS)
    for c in range(TM // CHUNK):
        rows = slice(c * CHUNK, (c + 1) * CHUNK)
        for hd in range(HEADS):
            cols = slice(hd * HEAD_DIM, (hd + 1) * HEAD_DIM)
            vn = (v_ref[rows, cols] * rinv[rows] * gv_ref[:, cols]).astype(BF16)
            sv = jnp.dot(ws_ref[hd], vn, preferred_element_type=F32) + bst_ref[:, hd:hd + 1]
            y_ref[rows, cols] = (u_ref[rows, cols] * sv).astype(BF16)
    xn = x + jnp.dot(y_ref[...], wout_ref[...], preferred_element_type=F32)
    xn_ref[...] = xn
    _route(xn, gffn_ref, wr_ref, br_ref, run_ref, h2_ref, ri_ref, cnt_ref)


def _pool_kernel(has_moe, *refs):
    n_in = 9 if has_moe else 3
    xin = refs[:n_in]
    (gmix_ref, win_ref, wg_ref, sc_ref, wout_ref, gffn_ref, wr_ref, br_ref,
     xn_ref, h2_ref, ri_ref, cnt_ref, y_ref, run_ref) = refs[n_in:]
    b = pl.program_id(0)
    i = pl.program_id(1)
    n_i = pl.num_programs(1)

    @pl.when((b == 0) & (i == 0))
    def _():
        run_ref[...] = jnp.zeros_like(run_ref)

    def load(k):
        if has_moe:
            return _combine(xin[k][...], xin[3 + k][...], xin[6 + k][...])
        return xin[k][...]

    x = load(0)
    g = gmix_ref[...]
    h = _rms(x, g).astype(BF16)
    xh = jnp.concatenate([load(1), load(2)], axis=0)
    hh = _rms(xh, g).astype(BF16)
    v = jnp.dot(h, win_ref[...], preferred_element_type=F32)
    vh = jnp.dot(hh, win_ref[...], preferred_element_type=F32)
    vb = v.astype(BF16)
    vhb = jnp.concatenate([vh, jnp.zeros((LANES - 2 * HALO, D), F32)], axis=0).astype(BF16)

    r = lax.broadcasted_iota(I32, (TM, TM), 0)
    m = lax.broadcasted_iota(I32, (TM, TM), 1)
    rh = lax.broadcasted_iota(I32, (TM, LANES), 0)
    ch = lax.broadcasted_iota(I32, (TM, LANES), 1)
    jh = jnp.where(ch < HALO, ch, ch + TM)
    okh = ((ch < HALO) & (i > 0)) | ((ch >= HALO) & (ch < 2 * HALO) & (i < n_i - 1))
    pos = (i * TM + lax.broadcasted_iota(I32, (TM, 1), 0))
    for gi, w in enumerate(POOL_WINDOWS):
        cols = slice(gi * GROUP_DIM, (gi + 1) * GROUP_DIM)
        half = w // 2
        pm = ((m >= r - half) & (m <= r + half - 1)).astype(BF16)
        ph = (okh & (jh >= rh + HALO - half) & (jh <= rh + HALO + half - 1)).astype(BF16)
        s = (jnp.dot(pm, vb[:, cols], preferred_element_type=F32)
             + jnp.dot(ph, vhb[:, cols], preferred_element_type=F32))
        lo = jnp.clip(pos - half, 0, SEQ)
        hi = jnp.clip(pos - half + w, 0, SEQ)
        cnt = (hi - lo).astype(F32)
        dlt = (s / cnt - v[:, cols]).astype(BF16)
        yg = jnp.dot(dlt, wg_ref[gi], preferred_element_type=F32)
        y_ref[:, cols] = (yg * sc_ref[:, cols]).astype(BF16)
    xn = x + jnp.dot(y_ref[...], wout_ref[...], preferred_element_type=F32)
    xn_ref[...] = xn
    _route(xn, gffn_ref, wr_ref, br_ref, run_ref, h2_ref, ri_ref, cnt_ref)


def _const_spec(shape):
    nd = len(shape)
    return pl.BlockSpec(shape, lambda *_: (0,) * nd)


def _mixer_out(n_steps_map):
    out_shape = (jax.ShapeDtypeStruct((T, D), F32), jax.ShapeDtypeStruct((T, D), F32),
                 jax.ShapeDtypeStruct((T, LANES), F32), jax.ShapeDtypeStruct((1, LANES), F32))
    out_specs = (pl.BlockSpec((TM, D), n_steps_map), pl.BlockSpec((TM, D), n_steps_map),
                 pl.BlockSpec((TM, LANES), n_steps_map), _const_spec((1, LANES)))
    return out_shape, out_specs


def _sgu_layer(xp, moe_in, gmix, win, b_in, gv, ws, bst, wout, gffn, wr, br):
    has_moe = moe_in is not None
    tile = lambda i: (i, 0)
    ins, specs = [xp], [pl.BlockSpec((TM, D), tile)]
    if has_moe:
        ins += list(moe_in)
        specs += [pl.BlockSpec((TM, 2 * D), tile), pl.BlockSpec((TM, LANES), tile)]
    weights = [gmix, win, b_in, gv, ws, bst, wout, gffn, wr, br]
    ins += weights
    specs += [_const_spec(w.shape) for w in weights]
    out_shape, out_specs = _mixer_out(tile)
    return pl.pallas_call(
        functools.partial(_sgu_kernel, has_moe),
        grid=(T // TM,),
        in_specs=specs,
        out_specs=out_specs,
        out_shape=out_shape,
        scratch_shapes=[pltpu.VMEM((TM, SGU_W), F32), pltpu.VMEM((TM, SGU_W), F32),
                        pltpu.VMEM((TM, SGU_W), BF16), pltpu.VMEM((1, LANES), F32)],
        compiler_params=pltpu.CompilerParams(dimension_semantics=("arbitrary",),
                                             vmem_limit_bytes=VMEM_LIMIT),
        name="sgu_mixer",
    )(*ins)


def _pool_layer(xp, moe_in, gmix, win, wg, sc, wout, gffn, wr, br):
    has_moe = moe_in is not None
    n_i = SEQ // TM
    tile = lambda b, i: (b * n_i + i, 0)
    per8 = TM // HALO
    prev = lambda b, i: (jnp.maximum(b * (SEQ // HALO) + i * per8 - 1, 0), 0)
    nxt = lambda b, i: (jnp.minimum(b * (SEQ // HALO) + (i + 1) * per8, T // HALO - 1), 0)
    arrays = [xp] + (list(moe_in) if has_moe else [])
    widths = [D] + ([2 * D, LANES] if has_moe else [])
    ins, specs = [], []
    for a, wdt in zip(arrays, widths):
        ins += [a, a, a]
        specs += [pl.BlockSpec((TM, wdt), tile), pl.BlockSpec((HALO, wdt), prev),
                  pl.BlockSpec((HALO, wdt), nxt)]
    weights = [gmix, win, wg, sc, wout, gffn, wr, br]
    ins += weights
    specs += [_const_spec(w.shape) for w in weights]
    out_shape, out_specs = _mixer_out(tile)
    return pl.pallas_call(
        functools.partial(_pool_kernel, has_moe),
        grid=(BATCH, n_i),
        in_specs=specs,
        out_specs=out_specs,
        out_shape=out_shape,
        scratch_shapes=[pltpu.VMEM((TM, D), BF16), pltpu.VMEM((1, LANES), F32)],
        compiler_params=pltpu.CompilerParams(dimension_semantics=("arbitrary", "arbitrary"),
                                             vmem_limit_bytes=VMEM_LIMIT),
        name="pool_mixer",
    )(*ins)


def _row_copy(src, dst, sem, s_row, d_row):
    return pltpu.make_async_copy(src.at[pl.ds(s_row, 1)], dst.at[pl.ds(d_row, 1)], sem)


def _scatter_kernel(dest_ref, h2_ref, xs_in_ref, xs_ref, sem):
    del xs_in_ref
    base = pl.program_id(0) * TD

    def issue(r, c):
        t = base + r
        _row_copy(h2_ref, xs_ref, sem, t, dest_ref[2 * t]).start()
        _row_copy(h2_ref, xs_ref, sem, t, dest_ref[2 * t + 1]).start()
        return c

    def drain(r, c):
        _row_copy(h2_ref, xs_ref, sem, 0, 0).wait()
        _row_copy(h2_ref, xs_ref, sem, 0, 0).wait()
        return c

    lax.fori_loop(0, TD, issue, 0)
    lax.fori_loop(0, TD, drain, 0)


def _gather_kernel(dest_ref, ys_ref, y2_ref, sem):
    base = pl.program_id(0) * (TOP_K * TD)

    def issue(r, c):
        a = base + r
        _row_copy(ys_ref, y2_ref, sem, dest_ref[a], a).start()
        return c

    def drain(r, c):
        _row_copy(ys_ref, y2_ref, sem, 0, 0).wait()
        return c

    lax.fori_loop(0, TOP_K * TD, issue, 0)
    lax.fori_loop(0, TOP_K * TD, drain, 0)


def _dispatch(dest, h2):
    any_spec = pl.BlockSpec(memory_space=pl.ANY)
    return pl.pallas_call(
        _scatter_kernel,
        grid_spec=pltpu.PrefetchScalarGridSpec(
            num_scalar_prefetch=1, grid=(T // TD,),
            in_specs=[any_spec, any_spec], out_specs=any_spec,
            scratch_shapes=[pltpu.SemaphoreType.DMA(())]),
        out_shape=jax.ShapeDtypeStruct((N_SLOTS, D), F32),
        input_output_aliases={2: 0},
        compiler_params=pltpu.CompilerParams(dimension_semantics=("arbitrary",)),
        name="moe_scatter",
    )(dest, h2, jnp.zeros((N_SLOTS, D), F32))


def _undispatch(dest, ys):
    any_spec = pl.BlockSpec(memory_space=pl.ANY)
    return pl.pallas_call(
        _gather_kernel,
        grid_spec=pltpu.PrefetchScalarGridSpec(
            num_scalar_prefetch=1, grid=(T // TD,),
            in_specs=[any_spec], out_specs=any_spec,
            scratch_shapes=[pltpu.SemaphoreType.DMA(())]),
        out_shape=jax.ShapeDtypeStruct((T * TOP_K, D), F32),
        compiler_params=pltpu.CompilerParams(dimension_semantics=("arbitrary",)),
        name="moe_gather",
    )(dest, ys)


def _expert_kernel(te_ref, nu_ref, xs_ref, wg_ref, wu_ref, wd_ref, ys_ref, wgb, wub, wdb):
    i = pl.program_id(0)
    changed = (i == 0) | (te_ref[i] != te_ref[jnp.maximum(i - 1, 0)])

    @pl.when(changed)
    def _():
        wgb[...] = wg_ref[...].astype(BF16)
        wub[...] = wu_ref[...].astype(BF16)
        wdb[...] = wd_ref[...].astype(BF16)

    @pl.when(i < nu_ref[0])
    def _():
        x = xs_ref[...].astype(BF16)
        a = jnp.dot(x, wgb[...], preferred_element_type=F32)
        u = jnp.dot(x, wub[...], preferred_element_type=F32)
        hid = (jax.nn.silu(a) * u).astype(BF16)
        ys_ref[...] = jnp.dot(hid, wdb[...], preferred_element_type=F32)

    @pl.when(i >= nu_ref[0])
    def _():
        ys_ref[...] = jnp.zeros_like(ys_ref)


def _experts(layer, tile_expert, n_used, xs, w_gate, w_up, w_down):
    wmap = lambda i, te, nu: (layer, te[i], 0, 0)
    tile = lambda i, te, nu: (i, 0)
    return pl.pallas_call(
        _expert_kernel,
        grid_spec=pltpu.PrefetchScalarGridSpec(
            num_scalar_prefetch=2, grid=(N_TILES,),
            in_specs=[pl.BlockSpec((TE, D), tile),
                      pl.BlockSpec((None, None, D, D_EXPERT), wmap),
                      pl.BlockSpec((None, None, D, D_EXPERT), wmap),
                      pl.BlockSpec((None, None, D_EXPERT, D), wmap)],
            out_specs=pl.BlockSpec((TE, D), tile),
            scratch_shapes=[pltpu.VMEM((D, D_EXPERT), BF16), pltpu.VMEM((D, D_EXPERT), BF16),
                            pltpu.VMEM((D_EXPERT, D), BF16)]),
        out_shape=jax.ShapeDtypeStruct((N_SLOTS, D), F32),
        compiler_params=pltpu.CompilerParams(dimension_semantics=("arbitrary",),
                                             vmem_limit_bytes=VMEM_LIMIT),
        name="moe_experts",
    )(tile_expert, n_used, xs, w_gate, w_up, w_down)


def _final_kernel(xp_ref, y2_ref, ri_ref, g_ref, o_ref):
    o_ref[...] = _rms(_combine(xp_ref[...], y2_ref[...], ri_ref[...]), g_ref[...])


def _final(xp, y2, ri, g):
    tile = lambda i: (i, 0)
    return pl.pallas_call(
        _final_kernel,
        grid=(T // TM,),
        in_specs=[pl.BlockSpec((TM, D), tile), pl.BlockSpec((TM, 2 * D), tile),
                  pl.BlockSpec((TM, LANES), tile), _const_spec((1, D))],
        out_specs=pl.BlockSpec((TM, D), tile),
        out_shape=jax.ShapeDtypeStruct((T, D), F32),
        compiler_params=pltpu.CompilerParams(dimension_semantics=("arbitrary",)),
        name="final_norm",
    )(xp, y2, ri, g)


def _plan(ri, cnt):
    eid = ri[:, 2:4].astype(I32)
    rank = ri[:, 4:6].astype(I32)
    counts = cnt[0, EXPERT_LANE0:EXPERT_LANE0 + N_EXPERTS].astype(I32)
    padded = ((counts + TE - 1) // TE) * TE
    pend = jnp.cumsum(padded)
    pstart = pend - padded
    sel = eid[:, :, None] == jnp.arange(N_EXPERTS, dtype=I32)
    dest = jnp.sum(jnp.where(sel, pstart, 0), axis=-1) + rank
    tile_start = jnp.arange(N_TILES, dtype=I32) * TE
    tile_expert = jnp.sum((tile_start[:, None] >= pend[None, :]).astype(I32), axis=-1)
    tile_expert = jnp.minimum(tile_expert, N_EXPERTS - 1)
    n_used = (pend[-1:] // TE).astype(I32)
    return dest.reshape(-1), tile_expert, n_used


def kernel(x, norm_mix_g, norm_ffn_g, a_w_in, a_b_in, a_norm_v, a_w_spatial, a_b_spatial, a_w_out, b_w_in, b_w_group, b_scale, b_w_out, router_group_w, router_group_b, router_expert_w, router_expert_b, w_gate, w_up, w_down, final_norm_g):
    xp = x.reshape(T, D)
    moe_in = None
    row = lambda a: a.reshape(1, -1)
    for i in range(DEPTH):
        j = i // 2
        wr = jnp.concatenate([router_group_w[i], router_expert_w[i]], axis=1)
        wr = jnp.pad(wr, ((0, 0), (0, LANES - wr.shape[1]))).astype(BF16)
        br = jnp.concatenate([router_group_b[i], router_expert_b[i]])
        br = jnp.pad(br, (0, LANES - br.shape[0])).reshape(1, LANES)
        gmix, gffn = row(norm_mix_g[i]), row(norm_ffn_g[i])
        if i % 2 == 0:
            xp, h2, ri, cnt = _sgu_layer(
                xp, moe_in, gmix, a_w_in[j].astype(BF16), row(a_b_in[j]), row(a_norm_v[j]),
                a_w_spatial[j].astype(BF16), a_b_spatial[j].T, a_w_out[j].astype(BF16),
                gffn, wr, br)
        else:
            xp, h2, ri, cnt = _pool_layer(
                xp, moe_in, gmix, b_w_in[j].astype(BF16), b_w_group[j].astype(BF16),
                row(b_scale[j]), b_w_out[j].astype(BF16), gffn, wr, br)
        dest, tile_expert, n_used = _plan(ri, cnt)
        xs = _dispatch(dest, h2)
        ys = _experts(i, tile_expert, n_used, xs, w_gate, w_up, w_down)
        y2 = _undispatch(dest, ys).reshape(T, TOP_K * D)
        moe_in = (y2, ri)
    out = _final(xp, moe_in[0], moe_in[1], row(final_norm_g))
    return out.reshape(BATCH, SEQ, D)
```

```python
import functools

import jax
import jax.numpy as jnp
from jax import lax
from jax.experimental import pallas as pl
from jax.experimental.pallas import tpu as pltpu

F32 = jnp.float32
BF16 = jnp.bfloat16
I32 = jnp.int32

D = 1024
BATCH = 16
SEQ = 2048
T = BATCH * SEQ
DEPTH = 4
CHUNK = 128
HEADS = 8
SGU_W = 2 * D
HEAD_DIM = SGU_W // HEADS
POOL_WINDOWS = (2, 4, 8, 16)
GROUP_DIM = D // len(POOL_WINDOWS)
N_GROUPS = 4
EPG = 8
N_EXPERTS = N_GROUPS * EPG
TOP_K = 2
D_EXPERT = D // 2
EPS = 1e-6

LANES = 128
SUBLANES = 8
TM = 256
TE = 256
N_TILES = (T * TOP_K) // TE + N_EXPERTS
N_SLOTS = N_TILES * TE
TD = 512
UNROLL = 8
HALO = SUBLANES
EXPERT_LANE0 = N_GROUPS
N_COL = 512
VMEM_LIMIT = 56 * 1024 * 1024


def _rms(x, g):
    return x * lax.rsqrt(jnp.mean(x * x, axis=-1, keepdims=True) + EPS) * g


def _combine(xp, y2, ri):
    return xp + (ri[:, 0:1] * y2[:, :D] + ri[:, 1:2] * y2[:, D:])


def _route(xn, gffn_ref, wr_ref, br_ref, run_ref, h2_ref, ri_ref, cnt_ref):
    tm = xn.shape[0]
    h2 = _rms(xn, gffn_ref[...])
    h2_ref[...] = h2
    logits = jnp.dot(h2.astype(BF16), wr_ref[...], preferred_element_type=F32) + br_ref[...]
    lane = lax.broadcasted_iota(I32, (tm, LANES), 1).astype(F32)
    big = float(LANES)

    def first_lane(mask):
        return jnp.min(jnp.where(mask, lane, big), axis=-1, keepdims=True)

    gmask = lane < float(N_GROUPS)
    gmax = jnp.max(jnp.where(gmask, logits, -jnp.inf), axis=-1, keepdims=True)
    gexp = jnp.where(gmask, jnp.exp(logits - gmax), 0.0)
    gprob = gexp / jnp.sum(gexp, axis=-1, keepdims=True)
    g_w = jnp.max(gprob, axis=-1, keepdims=True)
    g_idx = first_lane(gmask & (gprob == g_w))
    lo = float(EXPERT_LANE0) + float(EPG) * g_idx
    emask = (lane >= lo) & (lane < lo + float(EPG))
    emax = jnp.max(jnp.where(emask, logits, -jnp.inf), axis=-1, keepdims=True)
    eexp = jnp.where(emask, jnp.exp(logits - emax), 0.0)
    eprob = jnp.where(emask, eexp / jnp.sum(eexp, axis=-1, keepdims=True), -1.0)
    p1 = jnp.max(eprob, axis=-1, keepdims=True)
    i1 = first_lane(eprob == p1)
    eprob2 = jnp.where(lane == i1, -1.0, eprob)
    p2 = jnp.max(eprob2, axis=-1, keepdims=True)
    i2 = first_lane(eprob2 == p2)
    den = p1 + p2
    gate1 = g_w * (p1 / den)
    gate2 = g_w * (p2 / den)
    oh1 = lane == i1
    oh2 = lane == i2
    row = lax.broadcasted_iota(I32, (tm, tm), 0)
    col = lax.broadcasted_iota(I32, (tm, tm), 1)
    ltri = (col < row).astype(BF16)
    c1 = jnp.dot(ltri, oh1.astype(BF16), preferred_element_type=F32)
    c2 = jnp.dot(ltri, oh2.astype(BF16), preferred_element_type=F32)
    tot1 = jnp.sum(oh1.astype(F32), axis=0, keepdims=True)
    tot2 = jnp.sum(oh2.astype(F32), axis=0, keepdims=True)
    run = run_ref[...]
    r1 = jnp.sum(jnp.where(oh1, run + c1, 0.0), axis=-1, keepdims=True)
    r2 = jnp.sum(jnp.where(oh2, run + tot1 + c2, 0.0), axis=-1, keepdims=True)
    run = run + tot1 + tot2
    run_ref[...] = run
    cnt_ref[...] = run
    e0 = float(EXPERT_LANE0)
    vals = (gate1, gate2, i1 - e0, i2 - e0, r1, r2)
    ri = jnp.zeros((tm, LANES), F32)
    for k, v in enumerate(vals):
        ri = jnp.where(lane == float(k), v, ri)
    ri_ref[...] = ri


def _gelu_exact(z):
    return 0.5 * z * (1.0 + lax.erf(z * (0.5 ** 0.5)))


def _sgu_kernel(has_moe, *refs):
    if has_moe:
        xp_ref, y2_ref, rip_ref = refs[:3]
        refs = refs[3:]
    else:
        xp_ref = refs[0]
        refs = refs[1:]
    (gmix_ref, win_ref, bin_ref, gv_ref, ws_ref, bst_ref, wout_ref, gffn_ref, wr_ref, br_ref,
     xn_ref, h2_ref, ri_ref, cnt_ref, u_ref, v_ref, y_ref, run_ref) = refs

    @pl.when(pl.program_id(0) == 0)
    def _():
        run_ref[...] = jnp.zeros_like(run_ref)

    x = xp_ref[...]
    if has_moe:
        x = _combine(x, y2_ref[...], rip_ref[...])
    h = _rms(x, gmix_ref[...]).astype(BF16)
    ss = jnp.zeros((TM, 1), F32)
    for j in range(0, SGU_W, N_COL):
        z = jnp.dot(h, win_ref[:, j:j + N_COL], preferred_element_type=F32) + bin_ref[:, j:j + N_COL]
        u_ref[:, j:j + N_COL] = _gelu_exact(z)
    for j in range(0, SGU_W, N_COL):
        jj = SGU_W + j
        z = jnp.dot(h, win_ref[:, jj:jj + N_COL], preferred_element_type=F32) + bin_ref[:, jj:jj + N_COL]
        vv = _gelu_exact(z)
        v_ref[:, j:j + N_COL] = vv
        ss = ss + jnp.sum(vv * vv, axis=-1, keepdims=True)
    rinv = lax.rsqrt(ss * (1.0 / SGU_W) + EPS)
    for c in range(TM // CHUNK):
        rows = slice(c * CHUNK, (c + 1) * CHUNK)
        for hd in range(HEADS):
            cols = slice(hd * HEAD_DIM, (hd + 1) * HEAD_DIM)
            vn = (v_ref[rows, cols] * rinv[rows] * gv_ref[:, cols]).astype(BF16)
            sv = jnp.dot(ws_ref[hd], vn, preferred_element_type=F32) + bst_ref[:, hd:hd + 1]
            y_ref[rows, cols] = (u_ref[rows, cols] * sv).astype(BF16)
    xn = x + jnp.dot(y_ref[...], wout_ref[...], preferred_element_type=F32)
    xn_ref[...] = xn
    _route(xn, gffn_ref, wr_ref, br_ref, run_ref, h2_ref, ri_ref, cnt_ref)


def _pool_kernel(has_moe, *refs):
    n_in = 9 if has_moe else 3
    xin = refs[:n_in]
    (gmix_ref, win_ref, wg_ref, sc_ref, wout_ref, gffn_ref, wr_ref, br_ref,
     xn_ref, h2_ref, ri_ref, cnt_ref, y_ref, run_ref) = refs[n_in:]
    b = pl.program_id(0)
    i = pl.program_id(1)
    n_i = pl.num_programs(1)

    @pl.when((b == 0) & (i == 0))
    def _():
        run_ref[...] = jnp.zeros_like(run_ref)

    def load(k):
        if has_moe:
            return _combine(xin[k][...], xin[3 + k][...], xin[6 + k][...])
        return xin[k][...]

    x = load(0)
    g = gmix_ref[...]
    h = _rms(x, g).astype(BF16)
    xh = jnp.concatenate([load(1), load(2)], axis=0)
    hh = _rms(xh, g).astype(BF16)
    v = jnp.dot(h, win_ref[...], preferred_element_type=F32)
    vh = jnp.dot(hh, win_ref[...], preferred_element_type=F32)
    vb = v.astype(BF16)
    vhb = jnp.concatenate([vh, jnp.zeros((LANES - 2 * HALO, D), F32)], axis=0).astype(BF16)

    r = lax.broadcasted_iota(I32, (TM, TM), 0)
    m = lax.broadcasted_iota(I32, (TM, TM), 1)
    rh = lax.broadcasted_iota(I32, (TM, LANES), 0)
    ch = lax.broadcasted_iota(I32, (TM, LANES), 1)
    jh = jnp.where(ch < HALO, ch, ch + TM)
    okh = ((ch < HALO) & (i > 0)) | ((ch >= HALO) & (ch < 2 * HALO) & (i < n_i - 1))
    pos = (i * TM + lax.broadcasted_iota(I32, (TM, 1), 0))
    for gi, w in enumerate(POOL_WINDOWS):
        cols = slice(gi * GROUP_DIM, (gi + 1) * GROUP_DIM)
        half = w // 2
        pm = ((m >= r - half) & (m <= r + half - 1)).astype(BF16)
        ph = (okh & (jh >= rh + HALO - half) & (jh <= rh + HALO + half - 1)).astype(BF16)
        s = (jnp.dot(pm, vb[:, cols], preferred_element_type=F32)
             + jnp.dot(ph, vhb[:, cols], preferred_element_type=F32))
        lo = jnp.clip(pos - half, 0, SEQ)
        hi = jnp.clip(pos - half + w, 0, SEQ)
        cnt = (hi - lo).astype(F32)
        dlt = (s / cnt - v[:, cols]).astype(BF16)
        yg = jnp.dot(dlt, wg_ref[gi], preferred_element_type=F32)
        y_ref[:, cols] = (yg * sc_ref[:, cols]).astype(BF16)
    xn = x + jnp.dot(y_ref[...], wout_ref[...], preferred_element_type=F32)
    xn_ref[...] = xn
    _route(xn, gffn_ref, wr_ref, br_ref, run_ref, h2_ref, ri_ref, cnt_ref)


def _const_spec(shape):
    nd = len(shape)
    return pl.BlockSpec(shape, lambda *_: (0,) * nd)


def _mixer_out(n_steps_map):
    out_shape = (jax.ShapeDtypeStruct((T, D), F32), jax.ShapeDtypeStruct((T, D), F32),
                 jax.ShapeDtypeStruct((T, LANES), F32), jax.ShapeDtypeStruct((1, LANES), F32))
    out_specs = (pl.BlockSpec((TM, D), n_steps_map), pl.BlockSpec((TM, D), n_steps_map),
                 pl.BlockSpec((TM, LANES), n_steps_map), _const_spec((1, LANES)))
    return out_shape, out_specs


def _sgu_layer(xp, moe_in, gmix, win, b_in, gv, ws, bst, wout, gffn, wr, br):
    has_moe = moe_in is not None
    tile = lambda i: (i, 0)
    ins, specs = [xp], [pl.BlockSpec((TM, D), tile)]
    if has_moe:
        ins += list(moe_in)
        specs += [pl.BlockSpec((TM, 2 * D), tile), pl.BlockSpec((TM, LANES), tile)]
    weights = [gmix, win, b_in, gv, ws, bst, wout, gffn, wr, br]
    ins += weights
    specs += [_const_spec(w.shape) for w in weights]
    out_shape, out_specs = _mixer_out(tile)
    return pl.pallas_call(
        functools.partial(_sgu_kernel, has_moe),
        grid=(T // TM,),
        in_specs=specs,
        out_specs=out_specs,
        out_shape=out_shape,
        scratch_shapes=[pltpu.VMEM((TM, SGU_W), F32), pltpu.VMEM((TM, SGU_W), F32),
                        pltpu.VMEM((TM, SGU_W), BF16), pltpu.VMEM((1, LANES), F32)],
        compiler_params=pltpu.CompilerParams(dimension_semantics=("arbitrary",),
                                             vmem_limit_bytes=VMEM_LIMIT),
        name="sgu_mixer",
    )(*ins)


def _pool_layer(xp, moe_in, gmix, win, wg, sc, wout, gffn, wr, br):
    has_moe = moe_in is not None
    n_i = SEQ // TM
    tile = lambda b, i: (b * n_i + i, 0)
    per8 = TM // HALO
    prev = lambda b, i: (jnp.maximum(b * (SEQ // HALO) + i * per8 - 1, 0), 0)
    nxt = lambda b, i: (jnp.minimum(b * (SEQ // HALO) + (i + 1) * per8, T // HALO - 1), 0)
    arrays = [xp] + (list(moe_in) if has_moe else [])
    widths = [D] + ([2 * D, LANES] if has_moe else [])
    ins, specs = [], []
    for a, wdt in zip(arrays, widths):
        ins += [a, a, a]
        specs += [pl.BlockSpec((TM, wdt), tile), pl.BlockSpec((HALO, wdt), prev),
                  pl.BlockSpec((HALO, wdt), nxt)]
    weights = [gmix, win, wg, sc, wout, gffn, wr, br]
    ins += weights
    specs += [_const_spec(w.shape) for w in weights]
    out_shape, out_specs = _mixer_out(tile)
    return pl.pallas_call(
        functools.partial(_pool_kernel, has_moe),
        grid=(BATCH, n_i),
        in_specs=specs,
        out_specs=out_specs,
        out_shape=out_shape,
        scratch_shapes=[pltpu.VMEM((TM, D), BF16), pltpu.VMEM((1, LANES), F32)],
        compiler_params=pltpu.CompilerParams(dimension_semantics=("arbitrary", "arbitrary"),
                                             vmem_limit_bytes=VMEM_LIMIT),
        name="pool_mixer",
    )(*ins)


def _scatter_row(h2_ref, xs_ref, sem, r, d_row):
    return pltpu.make_async_copy(h2_ref.at[pl.ds(r, 1)], xs_ref.at[pl.ds(d_row, 1)], sem)


def _scatter_kernel(dest_ref, h2_ref, xs_in_ref, xs_ref, sem):
    del xs_in_ref
    base = pl.program_id(0) * TD

    def issue(q, c):
        for s in range(UNROLL):
            r = q * UNROLL + s
            for k in range(TOP_K):
                _scatter_row(h2_ref, xs_ref, sem, r, dest_ref[TOP_K * (base + r) + k]).start()
        return c

    def drain(q, c):
        for _ in range(UNROLL * TOP_K):
            _scatter_row(h2_ref, xs_ref, sem, 0, 0).wait()
        return c

    lax.fori_loop(0, TD // UNROLL, issue, 0)
    lax.fori_loop(0, TD // UNROLL, drain, 0)


def _gather_row(ys_ref, y2_ref, sem, s_row, r, k):
    return pltpu.make_async_copy(ys_ref.at[pl.ds(s_row, 1)],
                                 y2_ref.at[pl.ds(r, 1), pl.ds(k * D, D)], sem)


def _gather_kernel(dest_ref, ys_ref, y2_ref, sem):
    base = pl.program_id(0) * TD

    def issue(q, c):
        for s in range(UNROLL):
            r = q * UNROLL + s
            for k in range(TOP_K):
                _gather_row(ys_ref, y2_ref, sem, dest_ref[TOP_K * (base + r) + k], r, k).start()
        return c

    def drain(q, c):
        for _ in range(UNROLL):
            for k in range(TOP_K):
                _gather_row(ys_ref, y2_ref, sem, 0, 0, k).wait()
        return c

    lax.fori_loop(0, TD // UNROLL, issue, 0)
    lax.fori_loop(0, TD // UNROLL, drain, 0)


def _dispatch(dest, h2):
    any_spec = pl.BlockSpec(memory_space=pl.ANY)
    return pl.pallas_call(
        _scatter_kernel,
        grid_spec=pltpu.PrefetchScalarGridSpec(
            num_scalar_prefetch=1, grid=(T // TD,),
            in_specs=[pl.BlockSpec((TD, D), lambda i, d: (i, 0)), any_spec], out_specs=any_spec,
            scratch_shapes=[pltpu.SemaphoreType.DMA(())]),
        out_shape=jax.ShapeDtypeStruct((N_SLOTS, D), F32),
        input_output_aliases={2: 0},
        compiler_params=pltpu.CompilerParams(dimension_semantics=("arbitrary",)),
        name="moe_scatter",
    )(dest, h2, jnp.zeros((N_SLOTS, D), F32))


def _undispatch(dest, ys):
    any_spec = pl.BlockSpec(memory_space=pl.ANY)
    return pl.pallas_call(
        _gather_kernel,
        grid_spec=pltpu.PrefetchScalarGridSpec(
            num_scalar_prefetch=1, grid=(T // TD,),
            in_specs=[any_spec], out_specs=pl.BlockSpec((TD, TOP_K * D), lambda i, d: (i, 0)),
            scratch_shapes=[pltpu.SemaphoreType.DMA(())]),
        out_shape=jax.ShapeDtypeStruct((T, TOP_K * D), F32),
        compiler_params=pltpu.CompilerParams(dimension_semantics=("arbitrary",)),
        name="moe_gather",
    )(dest, ys)


def _expert_kernel(te_ref, nu_ref, xs_ref, wg_ref, wu_ref, wd_ref, ys_ref, wgb, wub, wdb):
    i = pl.program_id(0)
    changed = (i == 0) | (te_ref[i] != te_ref[jnp.maximum(i - 1, 0)])

    @pl.when(changed)
    def _():
        wgb[...] = wg_ref[...].astype(BF16)
        wub[...] = wu_ref[...].astype(BF16)
        wdb[...] = wd_ref[...].astype(BF16)

    @pl.when(i < nu_ref[0])
    def _():
        x = xs_ref[...].astype(BF16)
        a = jnp.dot(x, wgb[...], preferred_element_type=F32)
        u = jnp.dot(x, wub[...], preferred_element_type=F32)
        hid = (jax.nn.silu(a) * u).astype(BF16)
        ys_ref[...] = jnp.dot(hid, wdb[...], preferred_element_type=F32)

    @pl.when(i >= nu_ref[0])
    def _():
        ys_ref[...] = jnp.zeros_like(ys_ref)


def _experts(layer, tile_expert, n_used, xs, w_gate, w_up, w_down):
    wmap = lambda i, te, nu: (layer, te[i], 0, 0)
    tile = lambda i, te, nu: (i, 0)
    return pl.pallas_call(
        _expert_kernel,
        grid_spec=pltpu.PrefetchScalarGridSpec(
            num_scalar_prefetch=2, grid=(N_TILES,),
            in_specs=[pl.BlockSpec((TE, D), tile),
                      pl.BlockSpec((None, None, D, D_EXPERT), wmap),
                      pl.BlockSpec((None, None, D, D_EXPERT), wmap),
                      pl.BlockSpec((None, None, D_EXPERT, D), wmap)],
            out_specs=pl.BlockSpec((TE, D), tile),
            scratch_shapes=[pltpu.VMEM((D, D_EXPERT), BF16), pltpu.VMEM((D, D_EXPERT), BF16),
                            pltpu.VMEM((D_EXPERT, D), BF16)]),
        out_shape=jax.ShapeDtypeStruct((N_SLOTS, D), F32),
        compiler_params=pltpu.CompilerParams(dimension_semantics=("arbitrary",),
                                             vmem_limit_bytes=VMEM_LIMIT),
        name="moe_experts",
    )(tile_expert, n_used, xs, w_gate, w_up, w_down)


def _final_kernel(xp_ref, y2_ref, ri_ref, g_ref, o_ref):
    o_ref[...] = _rms(_combine(xp_ref[...], y2_ref[...], ri_ref[...]), g_ref[...])


def _final(xp, y2, ri, g):
    tile = lambda i: (i, 0)
    return pl.pallas_call(
        _final_kernel,
        grid=(T // TM,),
        in_specs=[pl.BlockSpec((TM, D), tile), pl.BlockSpec((TM, 2 * D), tile),
                  pl.BlockSpec((TM, LANES), tile), _const_spec((1, D))],
        out_specs=pl.BlockSpec((TM, D), tile),
        out_shape=jax.ShapeDtypeStruct((T, D), F32),
        compiler_params=pltpu.CompilerParams(dimension_semantics=("arbitrary",)),
        name="final_norm",
    )(xp, y2, ri, g)


def _plan(ri, cnt):
    eid = ri[:, 2:4].astype(I32)
    rank = ri[:, 4:6].astype(I32)
    counts = cnt[0, EXPERT_LANE0:EXPERT_LANE0 + N_EXPERTS].astype(I32)
    padded = ((counts + TE - 1) // TE) * TE
    pend = jnp.cumsum(padded)
    pstart = pend - padded
    sel = eid[:, :, None] == jnp.arange(N_EXPERTS, dtype=I32)
    dest = jnp.sum(jnp.where(sel, pstart, 0), axis=-1) + rank
    tile_start = jnp.arange(N_TILES, dtype=I32) * TE
    tile_expert = jnp.sum((tile_start[:, None] >= pend[None, :]).astype(I32), axis=-1)
    tile_expert = jnp.minimum(tile_expert, N_EXPERTS - 1)
    n_used = (pend[-1:] // TE).astype(I32)
    return dest.reshape(-1), tile_expert, n_used


def kernel(x, norm_mix_g, norm_ffn_g, a_w_in, a_b_in, a_norm_v, a_w_spatial, a_b_spatial, a_w_out, b_w_in, b_w_group, b_scale, b_w_out, router_group_w, router_group_b, router_expert_w, router_expert_b, w_gate, w_up, w_down, final_norm_g):
    xp = x.reshape(T, D)
    moe_in = None
    row = lambda a: a.reshape(1, -1)
    for i in range(DEPTH):
        j = i // 2
        wr = jnp.concatenate([router_group_w[i], router_expert_w[i]], axis=1)
        wr = jnp.pad(wr, ((0, 0), (0, LANES - wr.shape[1]))).astype(BF16)
        br = jnp.concatenate([router_group_b[i], router_expert_b[i]])
        br = jnp.pad(br, (0, LANES - br.shape[0])).reshape(1, LANES)
        gmix, gffn = row(norm_mix_g[i]), row(norm_ffn_g[i])
        if i % 2 == 0:
            xp, h2, ri, cnt = _sgu_layer(
                xp, moe_in, gmix, a_w_in[j].astype(BF16), row(a_b_in[j]), row(a_norm_v[j]),
                a_w_spatial[j].astype(BF16), a_b_spatial[j].T, a_w_out[j].astype(BF16),
                gffn, wr, br)
        else:
            xp, h2, ri, cnt = _pool_layer(
                xp, moe_in, gmix, b_w_in[j].astype(BF16), b_w_group[j].astype(BF16),
                row(b_scale[j]), b_w_out[j].astype(BF16), gffn, wr, br)
        dest, tile_expert, n_used = _plan(ri, cnt)
        xs = _dispatch(dest, h2)
        ys = _experts(i, tile_expert, n_used, xs, w_gate, w_up, w_down)
        y2 = _undispatch(dest, ys)
        moe_in = (y2, ri)
    out = _final(xp, moe_in[0], moe_in[1], row(final_norm_g))
    return out.reshape(BATCH, SEQ, D)
```

```python
import functools

import jax
import jax.numpy as jnp
from jax import lax
from jax.experimental import pallas as pl
from jax.experimental.pallas import tpu as pltpu

F32 = jnp.float32
BF16 = jnp.bfloat16
I32 = jnp.int32

D = 1024
BATCH = 16
SEQ = 2048
T = BATCH * SEQ
DEPTH = 4
CHUNK = 128
HEADS = 8
SGU_W = 2 * D
HEAD_DIM = SGU_W // HEADS
POOL_WINDOWS = (2, 4, 8, 16)
GROUP_DIM = D // len(POOL_WINDOWS)
N_GROUPS = 4
EPG = 8
N_EXPERTS = N_GROUPS * EPG
TOP_K = 2
D_EXPERT = D // 2
EPS = 1e-6

LANES = 128
SUBLANES = 8
TM = 256
TE = 256
N_TILES = (T * TOP_K) // TE + N_EXPERTS
N_SLOTS = N_TILES * TE
TD = 1024
UNROLL = 8
HALO = SUBLANES
EXPERT_LANE0 = N_GROUPS
N_COL = 512
VMEM_LIMIT = 56 * 1024 * 1024


def _rms(x, g):
    return x * lax.rsqrt(jnp.mean(x * x, axis=-1, keepdims=True) + EPS) * g


def _combine(xp, y2, ri):
    return xp + (ri[:, 0:1] * y2[:, :D] + ri[:, 1:2] * y2[:, D:])


def _route(xn, gffn_ref, wr_ref, br_ref, run_ref, h2_ref, ri_ref, cnt_ref):
    tm = xn.shape[0]
    h2 = _rms(xn, gffn_ref[...])
    h2_ref[...] = h2
    logits = jnp.dot(h2.astype(BF16), wr_ref[...], preferred_element_type=F32) + br_ref[...]
    lane = lax.broadcasted_iota(I32, (tm, LANES), 1).astype(F32)
    big = float(LANES)

    def first_lane(mask):
        return jnp.min(jnp.where(mask, lane, big), axis=-1, keepdims=True)

    gmask = lane < float(N_GROUPS)
    gmax = jnp.max(jnp.where(gmask, logits, -jnp.inf), axis=-1, keepdims=True)
    gexp = jnp.where(gmask, jnp.exp(logits - gmax), 0.0)
    gprob = gexp / jnp.sum(gexp, axis=-1, keepdims=True)
    g_w = jnp.max(gprob, axis=-1, keepdims=True)
    g_idx = first_lane(gmask & (gprob == g_w))
    lo = float(EXPERT_LANE0) + float(EPG) * g_idx
    emask = (lane >= lo) & (lane < lo + float(EPG))
    emax = jnp.max(jnp.where(emask, logits, -jnp.inf), axis=-1, keepdims=True)
    eexp = jnp.where(emask, jnp.exp(logits - emax), 0.0)
    eprob = jnp.where(emask, eexp / jnp.sum(eexp, axis=-1, keepdims=True), -1.0)
    p1 = jnp.max(eprob, axis=-1, keepdims=True)
    i1 = first_lane(eprob == p1)
    eprob2 = jnp.where(lane == i1, -1.0, eprob)
    p2 = jnp.max(eprob2, axis=-1, keepdims=True)
    i2 = first_lane(eprob2 == p2)
    den = p1 + p2
    gate1 = g_w * (p1 / den)
    gate2 = g_w * (p2 / den)
    oh1 = lane == i1
    oh2 = lane == i2
    row = lax.broadcasted_iota(I32, (tm, tm), 0)
    col = lax.broadcasted_iota(I32, (tm, tm), 1)
    ltri = (col < row).astype(BF16)
    c1 = jnp.dot(ltri, oh1.astype(BF16), preferred_element_type=F32)
    c2 = jnp.dot(ltri, oh2.astype(BF16), preferred_element_type=F32)
    tot1 = jnp.sum(oh1.astype(F32), axis=0, keepdims=True)
    tot2 = jnp.sum(oh2.astype(F32), axis=0, keepdims=True)
    run = run_ref[...]
    r1 = jnp.sum(jnp.where(oh1, run + c1, 0.0), axis=-1, keepdims=True)
    r2 = jnp.sum(jnp.where(oh2, run + tot1 + c2, 0.0), axis=-1, keepdims=True)
    run = run + tot1 + tot2
    run_ref[...] = run
    cnt_ref[...] = run
    e0 = float(EXPERT_LANE0)
    vals = (gate1, gate2, i1 - e0, i2 - e0, r1, r2)
    ri = jnp.zeros((tm, LANES), F32)
    for k, v in enumerate(vals):
        ri = jnp.where(lane == float(k), v, ri)
    ri_ref[...] = ri


def _gelu_exact(z):
    return 0.5 * z * (1.0 + lax.erf(z * (0.5 ** 0.5)))


def _sgu_kernel(has_moe, *refs):
    if has_moe:
        xp_ref, y2_ref, rip_ref = refs[:3]
        refs = refs[3:]
    else:
        xp_ref = refs[0]
        refs = refs[1:]
    (gmix_ref, win_ref, bin_ref, gv_ref, ws_ref, bst_ref, wout_ref, gffn_ref, wr_ref, br_ref,
     xn_ref, h2_ref, ri_ref, cnt_ref, u_ref, v_ref, y_ref, run_ref) = refs

    @pl.when(pl.program_id(0) == 0)
    def _():
        run_ref[...] = jnp.zeros_like(run_ref)

    x = xp_ref[...]
    if has_moe:
        x = _combine(x, y2_ref[...], rip_ref[...])
    h = _rms(x, gmix_ref[...]).astype(BF16)
    ss = jnp.zeros((TM, 1), F32)
    for j in range(0, SGU_W, N_COL):
        z = jnp.dot(h, win_ref[:, j:j + N_COL], preferred_element_type=F32) + bin_ref[:, j:j + N_COL]
        u_ref[:, j:j + N_COL] = _gelu_exact(z)
    for j in range(0, SGU_W, N_COL):
        jj = SGU_W + j
        z = jnp.dot(h, win_ref[:, jj:jj + N_COL], preferred_element_type=F32) + bin_ref[:, jj:jj + N_COL]
        vv = _gelu_exact(z)
        v_ref[:, j:j + N_COL] = vv
        ss = ss + jnp.sum(vv * vv, axis=-1, keepdims=True)
    rinv = lax.rsqrt(ss * (1.0 / SGU_W) + EPS)
    for c in range(TM // CHUNK):
        rows = slice(c * CHUNK, (c + 1) * CHUNK)
        for hd in range(HEADS):
            cols = slice(hd * HEAD_DIM, (hd + 1) * HEAD_DIM)
            vn = (v_ref[rows, cols] * rinv[rows] * gv_ref[:, cols]).astype(BF16)
            sv = jnp.dot(ws_ref[hd], vn, preferred_element_type=F32) + bst_ref[:, hd:hd + 1]
            y_ref[rows, cols] = (u_ref[rows, cols] * sv).astype(BF16)
    xn = x + jnp.dot(y_ref[...], wout_ref[...], preferred_element_type=F32)
    xn_ref[...] = xn
    _route(xn, gffn_ref, wr_ref, br_ref, run_ref, h2_ref, ri_ref, cnt_ref)


def _pool_kernel(has_moe, *refs):
    n_in = 9 if has_moe else 3
    xin = refs[:n_in]
    (gmix_ref, win_ref, wg_ref, sc_ref, wout_ref, gffn_ref, wr_ref, br_ref,
     xn_ref, h2_ref, ri_ref, cnt_ref, y_ref, run_ref) = refs[n_in:]
    b = pl.program_id(0)
    i = pl.program_id(1)
    n_i = pl.num_programs(1)

    @pl.when((b == 0) & (i == 0))
    def _():
        run_ref[...] = jnp.zeros_like(run_ref)

    def load(k):
        if has_moe:
            return _combine(xin[k][...], xin[3 + k][...], xin[6 + k][...])
        return xin[k][...]

    x = load(0)
    g = gmix_ref[...]
    h = _rms(x, g).astype(BF16)
    xh = jnp.concatenate([load(1), load(2)], axis=0)
    hh = _rms(xh, g).astype(BF16)
    v = jnp.dot(h, win_ref[...], preferred_element_type=F32)
    vh = jnp.dot(hh, win_ref[...], preferred_element_type=F32)
    vb = v.astype(BF16)
    vhb = jnp.concatenate([vh, jnp.zeros((LANES - 2 * HALO, D), F32)], axis=0).astype(BF16)

    r = lax.broadcasted_iota(I32, (TM, TM), 0)
    m = lax.broadcasted_iota(I32, (TM, TM), 1)
    rh = lax.broadcasted_iota(I32, (TM, LANES), 0)
    ch = lax.broadcasted_iota(I32, (TM, LANES), 1)
    jh = jnp.where(ch < HALO, ch, ch + TM)
    okh = ((ch < HALO) & (i > 0)) | ((ch >= HALO) & (ch < 2 * HALO) & (i < n_i - 1))
    pos = (i * TM + lax.broadcasted_iota(I32, (TM, 1), 0))
    for gi, w in enumerate(POOL_WINDOWS):
        cols = slice(gi * GROUP_DIM, (gi + 1) * GROUP_DIM)
        half = w // 2
        pm = ((m >= r - half) & (m <= r + half - 1)).astype(BF16)
        ph = (okh & (jh >= rh + HALO - half) & (jh <= rh + HALO + half - 1)).astype(BF16)
        s = (jnp.dot(pm, vb[:, cols], preferred_element_type=F32)
             + jnp.dot(ph, vhb[:, cols], preferred_element_type=F32))
        lo = jnp.clip(pos - half, 0, SEQ)
        hi = jnp.clip(pos - half + w, 0, SEQ)
        cnt = (hi - lo).astype(F32)
        dlt = (s / cnt - v[:, cols]).astype(BF16)
        yg = jnp.dot(dlt, wg_ref[gi], preferred_element_type=F32)
        y_ref[:, cols] = (yg * sc_ref[:, cols]).astype(BF16)
    xn = x + jnp.dot(y_ref[...], wout_ref[...], preferred_element_type=F32)
    xn_ref[...] = xn
    _route(xn, gffn_ref, wr_ref, br_ref, run_ref, h2_ref, ri_ref, cnt_ref)


def _const_spec(shape):
    nd = len(shape)
    return pl.BlockSpec(shape, lambda *_: (0,) * nd)


def _mixer_out(n_steps_map):
    out_shape = (jax.ShapeDtypeStruct((T, D), F32), jax.ShapeDtypeStruct((T, D), F32),
                 jax.ShapeDtypeStruct((T, LANES), F32), jax.ShapeDtypeStruct((1, LANES), F32))
    out_specs = (pl.BlockSpec((TM, D), n_steps_map), pl.BlockSpec((TM, D), n_steps_map),
                 pl.BlockSpec((TM, LANES), n_steps_map), _const_spec((1, LANES)))
    return out_shape, out_specs


def _sgu_layer(xp, moe_in, gmix, win, b_in, gv, ws, bst, wout, gffn, wr, br):
    has_moe = moe_in is not None
    tile = lambda i: (i, 0)
    ins, specs = [xp], [pl.BlockSpec((TM, D), tile)]
    if has_moe:
        ins += list(moe_in)
        specs += [pl.BlockSpec((TM, 2 * D), tile), pl.BlockSpec((TM, LANES), tile)]
    weights = [gmix, win, b_in, gv, ws, bst, wout, gffn, wr, br]
    ins += weights
    specs += [_const_spec(w.shape) for w in weights]
    out_shape, out_specs = _mixer_out(tile)
    return pl.pallas_call(
        functools.partial(_sgu_kernel, has_moe),
        grid=(T // TM,),
        in_specs=specs,
        out_specs=out_specs,
        out_shape=out_shape,
        scratch_shapes=[pltpu.VMEM((TM, SGU_W), F32), pltpu.VMEM((TM, SGU_W), F32),
                        pltpu.VMEM((TM, SGU_W), BF16), pltpu.VMEM((1, LANES), F32)],
        compiler_params=pltpu.CompilerParams(dimension_semantics=("arbitrary",),
                                             vmem_limit_bytes=VMEM_LIMIT),
        name="sgu_mixer",
    )(*ins)


def _pool_layer(xp, moe_in, gmix, win, wg, sc, wout, gffn, wr, br):
    has_moe = moe_in is not None
    n_i = SEQ // TM
    tile = lambda b, i: (b * n_i + i, 0)
    per8 = TM // HALO
    prev = lambda b, i: (jnp.maximum(b * (SEQ // HALO) + i * per8 - 1, 0), 0)
    nxt = lambda b, i: (jnp.minimum(b * (SEQ // HALO) + (i + 1) * per8, T // HALO - 1), 0)
    arrays = [xp] + (list(moe_in) if has_moe else [])
    widths = [D] + ([2 * D, LANES] if has_moe else [])
    ins, specs = [], []
    for a, wdt in zip(arrays, widths):
        ins += [a, a, a]
        specs += [pl.BlockSpec((TM, wdt), tile), pl.BlockSpec((HALO, wdt), prev),
                  pl.BlockSpec((HALO, wdt), nxt)]
    weights = [gmix, win, wg, sc, wout, gffn, wr, br]
    ins += weights
    specs += [_const_spec(w.shape) for w in weights]
    out_shape, out_specs = _mixer_out(tile)
    return pl.pallas_call(
        functools.partial(_pool_kernel, has_moe),
        grid=(BATCH, n_i),
        in_specs=specs,
        out_specs=out_specs,
        out_shape=out_shape,
        scratch_shapes=[pltpu.VMEM((TM, D), BF16), pltpu.VMEM((1, LANES), F32)],
        compiler_params=pltpu.CompilerParams(dimension_semantics=("arbitrary", "arbitrary"),
                                             vmem_limit_bytes=VMEM_LIMIT),
        name="pool_mixer",
    )(*ins)


def _gather_row(ys_ref, y2_ref, sem, s_row, r, k):
    return pltpu.make_async_copy(ys_ref.at[pl.ds(s_row, 1)],
                                 y2_ref.at[pl.ds(r, 1), pl.ds(k * D, D)], sem)


def _gather_kernel(dest_ref, ys_ref, y2_ref, sem):
    def issue(q, c):
        for s in range(UNROLL):
            r = q * UNROLL + s
            for k in range(TOP_K):
                _gather_row(ys_ref, y2_ref, sem, dest_ref[0, TOP_K * r + k], r, k).start(priority=k)
        return c

    def drain(q, c):
        for _ in range(UNROLL):
            for k in range(TOP_K):
                _gather_row(ys_ref, y2_ref, sem, 0, 0, k).wait()
        return c

    lax.fori_loop(0, TD // UNROLL, issue, 0)
    lax.fori_loop(0, TD // UNROLL, drain, 0)


def _dest_blocks(dest):
    spec = pl.BlockSpec((None, 1, TOP_K * TD), lambda i, *_: (i, 0, 0), memory_space=pltpu.SMEM)
    return dest.reshape(T // TD, 1, TOP_K * TD), spec


def _undispatch(dest, ys):
    dest3, dest_spec = _dest_blocks(dest)
    return pl.pallas_call(
        _gather_kernel,
        grid=(T // TD,),
        in_specs=[dest_spec, pl.BlockSpec(memory_space=pl.ANY)],
        out_specs=pl.BlockSpec((TD, TOP_K * D), lambda i: (i, 0)),
        scratch_shapes=[pltpu.SemaphoreType.DMA(())],
        out_shape=jax.ShapeDtypeStruct((T, TOP_K * D), F32),
        compiler_params=pltpu.CompilerParams(dimension_semantics=("arbitrary",),
                                             vmem_limit_bytes=VMEM_LIMIT),
        name="moe_gather",
    )(dest3, ys)


def _scatter_row(h2_ref, xs_ref, sem, r, d_row):
    return pltpu.make_async_copy(h2_ref.at[pl.ds(r, 1)], xs_ref.at[pl.ds(d_row, 1)], sem)


def _zero_tile(zbuf, xs_ref, sem, end_row):
    start = pl.multiple_of(end_row - TE, TE)
    return pltpu.make_async_copy(zbuf, xs_ref.at[pl.ds(start, TE)], sem)


def _scatter_kernel(pend_ref, dest_ref, h2_ref, xs_ref, zbuf, sem, zsem):
    @pl.when(pl.program_id(0) == 0)
    def _():
        zbuf[...] = jnp.zeros_like(zbuf)
        used_rows = pend_ref[N_EXPERTS - 1]
        for wait in (False, True):
            for e in range(N_EXPERTS):
                first = pend_ref[e - 1] if e else 0

                @pl.when(pend_ref[e] > first)
                def _():
                    cp = _zero_tile(zbuf, xs_ref, zsem, pend_ref[e])
                    cp.wait() if wait else cp.start()

            for j in range(1, N_TILES - (T * TOP_K) // TE + 1):
                end_row = used_rows + j * TE

                @pl.when(end_row <= N_SLOTS)
                def _():
                    cp = _zero_tile(zbuf, xs_ref, zsem, end_row)
                    cp.wait() if wait else cp.start()

    def issue(q, c):
        for s in range(UNROLL):
            r = q * UNROLL + s
            for k in range(TOP_K):
                _scatter_row(h2_ref, xs_ref, sem, r, dest_ref[0, TOP_K * r + k]).start(priority=k)
        return c

    def drain(q, c):
        for _ in range(UNROLL * TOP_K):
            _scatter_row(h2_ref, xs_ref, sem, 0, 0).wait()
        return c

    lax.fori_loop(0, TD // UNROLL, issue, 0)
    lax.fori_loop(0, TD // UNROLL, drain, 0)


def _dispatch(pend, dest, h2):
    dest3, dest_spec = _dest_blocks(dest)
    return pl.pallas_call(
        _scatter_kernel,
        grid_spec=pltpu.PrefetchScalarGridSpec(
            num_scalar_prefetch=1, grid=(T // TD,),
            in_specs=[dest_spec, pl.BlockSpec((TD, D), lambda i, p: (i, 0))],
            out_specs=pl.BlockSpec(memory_space=pl.ANY),
            scratch_shapes=[pltpu.VMEM((TE, D), F32), pltpu.SemaphoreType.DMA(()),
                            pltpu.SemaphoreType.DMA(())]),
        out_shape=jax.ShapeDtypeStruct((N_SLOTS, D), F32),
        compiler_params=pltpu.CompilerParams(dimension_semantics=("arbitrary",),
                                             vmem_limit_bytes=VMEM_LIMIT),
        name="moe_scatter",
    )(pend, dest3, h2)


def _expert_kernel(te_ref, nu_ref, xs_ref, wg_ref, wu_ref, wd_ref, ys_ref, wgb, wub, wdb):
    i = pl.program_id(0)
    changed = (i == 0) | (te_ref[i] != te_ref[jnp.maximum(i - 1, 0)])

    @pl.when(changed)
    def _():
        wgb[...] = wg_ref[...].astype(BF16)
        wub[...] = wu_ref[...].astype(BF16)
        wdb[...] = wd_ref[...].astype(BF16)

    @pl.when(i < nu_ref[0])
    def _():
        x = xs_ref[...].astype(BF16)
        a = jnp.dot(x, wgb[...], preferred_element_type=F32)
        u = jnp.dot(x, wub[...], preferred_element_type=F32)
        hid = (jax.nn.silu(a) * u).astype(BF16)
        ys_ref[...] = jnp.dot(hid, wdb[...], preferred_element_type=F32)

    @pl.when(i >= nu_ref[0])
    def _():
        ys_ref[...] = jnp.zeros_like(ys_ref)


def _experts(layer, tile_expert, n_used, xs, w_gate, w_up, w_down):
    wmap = lambda i, te, nu: (layer, te[i], 0, 0)
    tile = lambda i, te, nu: (i, 0)
    used_tile = lambda i, te, nu: (jnp.minimum(i, nu[0] - 1), 0)
    return pl.pallas_call(
        _expert_kernel,
        grid_spec=pltpu.PrefetchScalarGridSpec(
            num_scalar_prefetch=2, grid=(N_TILES,),
            in_specs=[pl.BlockSpec((TE, D), used_tile),
                      pl.BlockSpec((None, None, D, D_EXPERT), wmap),
                      pl.BlockSpec((None, None, D, D_EXPERT), wmap),
                      pl.BlockSpec((None, None, D_EXPERT, D), wmap)],
            out_specs=pl.BlockSpec((TE, D), tile),
            scratch_shapes=[pltpu.VMEM((D, D_EXPERT), BF16), pltpu.VMEM((D, D_EXPERT), BF16),
                            pltpu.VMEM((D_EXPERT, D), BF16)]),
        out_shape=jax.ShapeDtypeStruct((N_SLOTS, D), F32),
        compiler_params=pltpu.CompilerParams(dimension_semantics=("arbitrary",),
                                             vmem_limit_bytes=VMEM_LIMIT),
        name="moe_experts",
    )(tile_expert, n_used, xs, w_gate, w_up, w_down)


def _final_kernel(xp_ref, y2_ref, ri_ref, g_ref, o_ref):
    o_ref[...] = _rms(_combine(xp_ref[...], y2_ref[...], ri_ref[...]), g_ref[...])


def _final(xp, y2, ri, g):
    tile = lambda i: (i, 0)
    return pl.pallas_call(
        _final_kernel,
        grid=(T // TM,),
        in_specs=[pl.BlockSpec((TM, D), tile), pl.BlockSpec((TM, 2 * D), tile),
                  pl.BlockSpec((TM, LANES), tile), _const_spec((1, D))],
        out_specs=pl.BlockSpec((TM, D), tile),
        out_shape=jax.ShapeDtypeStruct((T, D), F32),
        compiler_params=pltpu.CompilerParams(dimension_semantics=("arbitrary",)),
        name="final_norm",
    )(xp, y2, ri, g)


def _plan(ri, cnt):
    eid = ri[:, 2:4].astype(I32)
    rank = ri[:, 4:6].astype(I32)
    counts = cnt[0, EXPERT_LANE0:EXPERT_LANE0 + N_EXPERTS].astype(I32)
    padded = ((counts + TE - 1) // TE) * TE
    pend = jnp.cumsum(padded)
    pstart = pend - padded
    sel = eid[:, :, None] == jnp.arange(N_EXPERTS, dtype=I32)
    dest = jnp.sum(jnp.where(sel, pstart, 0), axis=-1) + rank
    tile_start = jnp.arange(N_TILES, dtype=I32) * TE
    tile_expert = jnp.sum((tile_start[:, None] >= pend[None, :]).astype(I32), axis=-1)
    tile_expert = jnp.minimum(tile_expert, N_EXPERTS - 1)
    n_used = (pend[-1:] // TE).astype(I32)
    return dest.reshape(-1), tile_expert, n_used, pend.astype(I32)


def kernel(x, norm_mix_g, norm_ffn_g, a_w_in, a_b_in, a_norm_v, a_w_spatial, a_b_spatial, a_w_out, b_w_in, b_w_group, b_scale, b_w_out, router_group_w, router_group_b, router_expert_w, router_expert_b, w_gate, w_up, w_down, final_norm_g):
    xp = x.reshape(T, D)
    moe_in = None
    row = lambda a: a.reshape(1, -1)
    for i in range(DEPTH):
        j = i // 2
        wr = jnp.concatenate([router_group_w[i], router_expert_w[i]], axis=1)
        wr = jnp.pad(wr, ((0, 0), (0, LANES - wr.shape[1]))).astype(BF16)
        br = jnp.concatenate([router_group_b[i], router_expert_b[i]])
        br = jnp.pad(br, (0, LANES - br.shape[0])).reshape(1, LANES)
        gmix, gffn = row(norm_mix_g[i]), row(norm_ffn_g[i])
        if i % 2 == 0:
            xp, h2, ri, cnt = _sgu_layer(
                xp, moe_in, gmix, a_w_in[j].astype(BF16), row(a_b_in[j]), row(a_norm_v[j]),
                a_w_spatial[j].astype(BF16), a_b_spatial[j].T, a_w_out[j].astype(BF16),
                gffn, wr, br)
        else:
            xp, h2, ri, cnt = _pool_layer(
                xp, moe_in, gmix, b_w_in[j].astype(BF16), b_w_group[j].astype(BF16),
                row(b_scale[j]), b_w_out[j].astype(BF16), gffn, wr, br)
        dest, tile_expert, n_used, pend = _plan(ri, cnt)
        xs = _dispatch(pend, dest, h2)
        ys = _experts(i, tile_expert, n_used, xs, w_gate, w_up, w_down)
        y2 = _undispatch(dest, ys)
        moe_in = (y2, ri)
    out = _final(xp, moe_in[0], moe_in[1], row(final_norm_g))
    return out.reshape(BATCH, SEQ, D)
```

```python
import functools

import jax
import jax.numpy as jnp
from jax import lax
from jax.experimental import pallas as pl
from jax.experimental.pallas import tpu as pltpu

F32 = jnp.float32
BF16 = jnp.bfloat16
I32 = jnp.int32

D = 1024
BATCH = 16
SEQ = 2048
T = BATCH * SEQ
DEPTH = 4
CHUNK = 128
HEADS = 8
SGU_W = 2 * D
HEAD_DIM = SGU_W // HEADS
POOL_WINDOWS = (2, 4, 8, 16)
GROUP_DIM = D // len(POOL_WINDOWS)
N_GROUPS = 4
EPG = 8
N_EXPERTS = N_GROUPS * EPG
TOP_K = 2
D_EXPERT = D // 2
EPS = 1e-6

LANES = 128
SUBLANES = 8
TM = 256
TE = 256
N_TILES = (T * TOP_K) // TE + N_EXPERTS
N_SLOTS = N_TILES * TE
TD = 1024
HALO = SUBLANES
EXPERT_LANE0 = N_GROUPS
N_COL = 512
VMEM_LIMIT = 56 * 1024 * 1024


def _rms(x, g):
    return x * lax.rsqrt(jnp.mean(x * x, axis=-1, keepdims=True) + EPS) * g


def _combine(xp, y2, ri):
    return xp + (ri[:, 0:1] * y2[:, :D] + ri[:, 1:2] * y2[:, D:])


def _route(xn, gffn_ref, wr_ref, br_ref, run_ref, h2_ref, ri_ref, cnt_ref):
    tm = xn.shape[0]
    h2 = _rms(xn, gffn_ref[...])
    h2_ref[...] = h2
    logits = jnp.dot(h2.astype(BF16), wr_ref[...], preferred_element_type=F32) + br_ref[...]
    lane = lax.broadcasted_iota(I32, (tm, LANES), 1).astype(F32)
    big = float(LANES)

    def first_lane(mask):
        return jnp.min(jnp.where(mask, lane, big), axis=-1, keepdims=True)

    gmask = lane < float(N_GROUPS)
    gmax = jnp.max(jnp.where(gmask, logits, -jnp.inf), axis=-1, keepdims=True)
    gexp = jnp.where(gmask, jnp.exp(logits - gmax), 0.0)
    gprob = gexp / jnp.sum(gexp, axis=-1, keepdims=True)
    g_w = jnp.max(gprob, axis=-1, keepdims=True)
    g_idx = first_lane(gmask & (gprob == g_w))
    lo = float(EXPERT_LANE0) + float(EPG) * g_idx
    emask = (lane >= lo) & (lane < lo + float(EPG))
    emax = jnp.max(jnp.where(emask, logits, -jnp.inf), axis=-1, keepdims=True)
    eexp = jnp.where(emask, jnp.exp(logits - emax), 0.0)
    eprob = jnp.where(emask, eexp / jnp.sum(eexp, axis=-1, keepdims=True), -1.0)
    p1 = jnp.max(eprob, axis=-1, keepdims=True)
    i1 = first_lane(eprob == p1)
    eprob2 = jnp.where(lane == i1, -1.0, eprob)
    p2 = jnp.max(eprob2, axis=-1, keepdims=True)
    i2 = first_lane(eprob2 == p2)
    den = p1 + p2
    gate1 = g_w * (p1 / den)
    gate2 = g_w * (p2 / den)
    oh1 = lane == i1
    oh2 = lane == i2
    row = lax.broadcasted_iota(I32, (tm, tm), 0)
    col = lax.broadcasted_iota(I32, (tm, tm), 1)
    ltri = (col < row).astype(BF16)
    c1 = jnp.dot(ltri, oh1.astype(BF16), preferred_element_type=F32)
    c2 = jnp.dot(ltri, oh2.astype(BF16), preferred_element_type=F32)
    tot1 = jnp.sum(oh1.astype(F32), axis=0, keepdims=True)
    tot2 = jnp.sum(oh2.astype(F32), axis=0, keepdims=True)
    run = run_ref[...]
    r1 = jnp.sum(jnp.where(oh1, run + c1, 0.0), axis=-1, keepdims=True)
    r2 = jnp.sum(jnp.where(oh2, run + tot1 + c2, 0.0), axis=-1, keepdims=True)
    run = run + tot1 + tot2
    run_ref[...] = run
    cnt_ref[...] = run
    e0 = float(EXPERT_LANE0)
    vals = (gate1, gate2, i1 - e0, i2 - e0, r1, r2)
    ri = jnp.zeros((tm, LANES), F32)
    for k, v in enumerate(vals):
        ri = jnp.where(lane == float(k), v, ri)
    ri_ref[...] = ri


def _gelu_exact(z):
    return 0.5 * z * (1.0 + lax.erf(z * (0.5 ** 0.5)))


def _sgu_kernel(has_moe, *refs):
    if has_moe:
        xp_ref, y2_ref, rip_ref = refs[:3]
        refs = refs[3:]
    else:
        xp_ref = refs[0]
        refs = refs[1:]
    (gmix_ref, win_ref, bin_ref, gv_ref, ws_ref, bst_ref, wout_ref, gffn_ref, wr_ref, br_ref,
     xn_ref, h2_ref, ri_ref, cnt_ref, u_ref, v_ref, y_ref, run_ref) = refs

    @pl.when(pl.program_id(0) == 0)
    def _():
        run_ref[...] = jnp.zeros_like(run_ref)

    x = xp_ref[...]
    if has_moe:
        x = _combine(x, y2_ref[...], rip_ref[...])
    h = _rms(x, gmix_ref[...]).astype(BF16)
    ss = jnp.zeros((TM, 1), F32)
    for j in range(0, SGU_W, N_COL):
        z = jnp.dot(h, win_ref[:, j:j + N_COL], preferred_element_type=F32) + bin_ref[:, j:j + N_COL]
        u_ref[:, j:j + N_COL] = _gelu_exact(z)
    for j in range(0, SGU_W, N_COL):
        jj = SGU_W + j
        z = jnp.dot(h, win_ref[:, jj:jj + N_COL], preferred_element_type=F32) + bin_ref[:, jj:jj + N_COL]
        vv = _gelu_exact(z)
        v_ref[:, j:j + N_COL] = vv
        ss = ss + jnp.sum(vv * vv, axis=-1, keepdims=True)
    rinv = lax.rsqrt(ss * (1.0 / SGU_W) + EPS)
    for c in range(TM // CHUNK):
        rows = slice(c * CHUNK, (c + 1) * CHUNK)
        for hd in range(HEADS):
            cols = slice(hd * HEAD_DIM, (hd + 1) * HEAD_DIM)
            vn = (v_ref[rows, cols] * rinv[rows] * gv_ref[:, cols]).astype(BF16)
            sv = jnp.dot(ws_ref[hd], vn, preferred_element_type=F32) + bst_ref[:, hd:hd + 1]
            y_ref[rows, cols] = (u_ref[rows, cols] * sv).astype(BF16)
    xn = x + jnp.dot(y_ref[...], wout_ref[...], preferred_element_type=F32)
    xn_ref[...] = xn
    _route(xn, gffn_ref, wr_ref, br_ref, run_ref, h2_ref, ri_ref, cnt_ref)


def _pool_kernel(has_moe, *refs):
    n_in = 9 if has_moe else 3
    xin = refs[:n_in]
    (gmix_ref, win_ref, wg_ref, sc_ref, wout_ref, gffn_ref, wr_ref, br_ref,
     xn_ref, h2_ref, ri_ref, cnt_ref, y_ref, run_ref) = refs[n_in:]
    b = pl.program_id(0)
    i = pl.program_id(1)
    n_i = pl.num_programs(1)

    @pl.when((b == 0) & (i == 0))
    def _():
        run_ref[...] = jnp.zeros_like(run_ref)

    def load(k):
        if has_moe:
            return _combine(xin[k][...], xin[3 + k][...], xin[6 + k][...])
        return xin[k][...]

    x = load(0)
    g = gmix_ref[...]
    h = _rms(x, g).astype(BF16)
    xh = jnp.concatenate([load(1), load(2)], axis=0)
    hh = _rms(xh, g).astype(BF16)
    v = jnp.dot(h, win_ref[...], preferred_element_type=F32)
    vh = jnp.dot(hh, win_ref[...], preferred_element_type=F32)
    vb = v.astype(BF16)
    vhb = jnp.concatenate([vh, jnp.zeros((LANES - 2 * HALO, D), F32)], axis=0).astype(BF16)

    r = lax.broadcasted_iota(I32, (TM, TM), 0)
    m = lax.broadcasted_iota(I32, (TM, TM), 1)
    rh = lax.broadcasted_iota(I32, (TM, LANES), 0)
    ch = lax.broadcasted_iota(I32, (TM, LANES), 1)
    jh = jnp.where(ch < HALO, ch, ch + TM)
    okh = ((ch < HALO) & (i > 0)) | ((ch >= HALO) & (ch < 2 * HALO) & (i < n_i - 1))
    pos = (i * TM + lax.broadcasted_iota(I32, (TM, 1), 0))
    for gi, w in enumerate(POOL_WINDOWS):
        cols = slice(gi * GROUP_DIM, (gi + 1) * GROUP_DIM)
        half = w // 2
        pm = ((m >= r - half) & (m <= r + half - 1)).astype(BF16)
        ph = (okh & (jh >= rh + HALO - half) & (jh <= rh + HALO + half - 1)).astype(BF16)
        s = (jnp.dot(pm, vb[:, cols], preferred_element_type=F32)
             + jnp.dot(ph, vhb[:, cols], preferred_element_type=F32))
        lo = jnp.clip(pos - half, 0, SEQ)
        hi = jnp.clip(pos - half + w, 0, SEQ)
        cnt = (hi - lo).astype(F32)
        dlt = (s / cnt - v[:, cols]).astype(BF16)
        yg = jnp.dot(dlt, wg_ref[gi], preferred_element_type=F32)
        y_ref[:, cols] = (yg * sc_ref[:, cols]).astype(BF16)
    xn = x + jnp.dot(y_ref[...], wout_ref[...], preferred_element_type=F32)
    xn_ref[...] = xn
    _route(xn, gffn_ref, wr_ref, br_ref, run_ref, h2_ref, ri_ref, cnt_ref)


def _const_spec(shape):
    nd = len(shape)
    return pl.BlockSpec(shape, lambda *_: (0,) * nd)


def _mixer_out(n_steps_map):
    out_shape = (jax.ShapeDtypeStruct((T, D), F32), jax.ShapeDtypeStruct((T, D), F32),
                 jax.ShapeDtypeStruct((T, LANES), F32), jax.ShapeDtypeStruct((1, LANES), F32))
    out_specs = (pl.BlockSpec((TM, D), n_steps_map), pl.BlockSpec((TM, D), n_steps_map),
                 pl.BlockSpec((TM, LANES), n_steps_map), _const_spec((1, LANES)))
    return out_shape, out_specs


def _sgu_layer(xp, moe_in, gmix, win, b_in, gv, ws, bst, wout, gffn, wr, br):
    has_moe = moe_in is not None
    tile = lambda i: (i, 0)
    ins, specs = [xp], [pl.BlockSpec((TM, D), tile)]
    if has_moe:
        ins += list(moe_in)
        specs += [pl.BlockSpec((TM, 2 * D), tile), pl.BlockSpec((TM, LANES), tile)]
    weights = [gmix, win, b_in, gv, ws, bst, wout, gffn, wr, br]
    ins += weights
    specs += [_const_spec(w.shape) for w in weights]
    out_shape, out_specs = _mixer_out(tile)
    return pl.pallas_call(
        functools.partial(_sgu_kernel, has_moe),
        grid=(T // TM,),
        in_specs=specs,
        out_specs=out_specs,
        out_shape=out_shape,
        scratch_shapes=[pltpu.VMEM((TM, SGU_W), F32), pltpu.VMEM((TM, SGU_W), F32),
                        pltpu.VMEM((TM, SGU_W), BF16), pltpu.VMEM((1, LANES), F32)],
        compiler_params=pltpu.CompilerParams(dimension_semantics=("arbitrary",),
                                             vmem_limit_bytes=VMEM_LIMIT),
        name="sgu_mixer",
    )(*ins)


def _pool_layer(xp, moe_in, gmix, win, wg, sc, wout, gffn, wr, br):
    has_moe = moe_in is not None
    n_i = SEQ // TM
    tile = lambda b, i: (b * n_i + i, 0)
    per8 = TM // HALO
    prev = lambda b, i: (jnp.maximum(b * (SEQ // HALO) + i * per8 - 1, 0), 0)
    nxt = lambda b, i: (jnp.minimum(b * (SEQ // HALO) + (i + 1) * per8, T // HALO - 1), 0)
    arrays = [xp] + (list(moe_in) if has_moe else [])
    widths = [D] + ([2 * D, LANES] if has_moe else [])
    ins, specs = [], []
    for a, wdt in zip(arrays, widths):
        ins += [a, a, a]
        specs += [pl.BlockSpec((TM, wdt), tile), pl.BlockSpec((HALO, wdt), prev),
                  pl.BlockSpec((HALO, wdt), nxt)]
    weights = [gmix, win, wg, sc, wout, gffn, wr, br]
    ins += weights
    specs += [_const_spec(w.shape) for w in weights]
    out_shape, out_specs = _mixer_out(tile)
    return pl.pallas_call(
        functools.partial(_pool_kernel, has_moe),
        grid=(BATCH, n_i),
        in_specs=specs,
        out_specs=out_specs,
        out_shape=out_shape,
        scratch_shapes=[pltpu.VMEM((TM, D), BF16), pltpu.VMEM((1, LANES), F32)],
        compiler_params=pltpu.CompilerParams(dimension_semantics=("arbitrary", "arbitrary"),
                                             vmem_limit_bytes=VMEM_LIMIT),
        name="pool_mixer",
    )(*ins)


def _gather_row(ys_ref, y2_ref, sem, s_row, q, s, k):
    return pltpu.make_async_copy(ys_ref.at[pl.ds(s_row, 1)],
                                 y2_ref.at[q, pl.ds(s, 1), pl.ds(k * D, D)], sem)


def _gather_kernel(dest_ref, ys_ref, y2_ref, sem):
    def issue(q, c):
        for s in range(SUBLANES):
            for k in range(TOP_K):
                idx = dest_ref[0, q * (TOP_K * SUBLANES) + (TOP_K * s + k)]
                _gather_row(ys_ref, y2_ref, sem, idx, q, s, k).start(priority=k)
        return c

    def drain(q, c):
        for s in range(SUBLANES):
            for k in range(TOP_K):
                _gather_row(ys_ref, y2_ref, sem, 0, 0, s, k).wait()
        return c

    lax.fori_loop(0, TD // SUBLANES, issue, 0)
    lax.fori_loop(0, TD // SUBLANES, drain, 0)


def _dest_blocks(dest):
    spec = pl.BlockSpec((None, 1, TOP_K * TD), lambda i, *_: (i, 0, 0), memory_space=pltpu.SMEM)
    return dest.reshape(T // TD, 1, TOP_K * TD), spec


def _undispatch(dest, ys):
    dest3, dest_spec = _dest_blocks(dest)
    return pl.pallas_call(
        _gather_kernel,
        grid=(T // TD,),
        in_specs=[dest_spec, pl.BlockSpec(memory_space=pl.ANY)],
        out_specs=pl.BlockSpec((TD // SUBLANES, SUBLANES, TOP_K * D), lambda i: (i, 0, 0)),
        scratch_shapes=[pltpu.SemaphoreType.DMA(())],
        out_shape=jax.ShapeDtypeStruct((T // SUBLANES, SUBLANES, TOP_K * D), F32),
        compiler_params=pltpu.CompilerParams(dimension_semantics=("arbitrary",),
                                             vmem_limit_bytes=VMEM_LIMIT),
        name="moe_gather",
    )(dest3, ys).reshape(T, TOP_K * D)


def _scatter_row(h2_ref, xs_ref, sem, q, s, d_row):
    return pltpu.make_async_copy(h2_ref.at[q, pl.ds(s, 1)], xs_ref.at[pl.ds(d_row, 1)], sem)


def _zero_tile(zbuf, xs_ref, sem, end_row):
    start = pl.multiple_of(end_row - TE, TE)
    return pltpu.make_async_copy(zbuf, xs_ref.at[pl.ds(start, TE)], sem)


def _scatter_kernel(pend_ref, dest_ref, h2_ref, xs_ref, zbuf, sem, zsem):
    @pl.when(pl.program_id(0) == 0)
    def _():
        zbuf[...] = jnp.zeros_like(zbuf)
        used_rows = pend_ref[N_EXPERTS - 1]
        for wait in (False, True):
            for e in range(N_EXPERTS):
                first = pend_ref[e - 1] if e else 0

                @pl.when(pend_ref[e] > first)
                def _():
                    cp = _zero_tile(zbuf, xs_ref, zsem, pend_ref[e])
                    cp.wait() if wait else cp.start()

            for j in range(1, N_TILES - (T * TOP_K) // TE + 1):
                end_row = used_rows + j * TE

                @pl.when(end_row <= N_SLOTS)
                def _():
                    cp = _zero_tile(zbuf, xs_ref, zsem, end_row)
                    cp.wait() if wait else cp.start()

    def issue(q, c):
        for s in range(SUBLANES):
            for k in range(TOP_K):
                idx = dest_ref[0, q * (TOP_K * SUBLANES) + (TOP_K * s + k)]
                _scatter_row(h2_ref, xs_ref, sem, q, s, idx).start(priority=k)
        return c

    def drain(q, c):
        for s in range(SUBLANES):
            for _ in range(TOP_K):
                _scatter_row(h2_ref, xs_ref, sem, 0, s, 0).wait()
        return c

    lax.fori_loop(0, TD // SUBLANES, issue, 0)
    lax.fori_loop(0, TD // SUBLANES, drain, 0)


def _dispatch(pend, dest, h2):
    dest3, dest_spec = _dest_blocks(dest)
    return pl.pallas_call(
        _scatter_kernel,
        grid_spec=pltpu.PrefetchScalarGridSpec(
            num_scalar_prefetch=1, grid=(T // TD,),
            in_specs=[dest_spec,
                      pl.BlockSpec((TD // SUBLANES, SUBLANES, D), lambda i, p: (i, 0, 0))],
            out_specs=pl.BlockSpec(memory_space=pl.ANY),
            scratch_shapes=[pltpu.VMEM((TE, D), F32), pltpu.SemaphoreType.DMA(()),
                            pltpu.SemaphoreType.DMA(())]),
        out_shape=jax.ShapeDtypeStruct((N_SLOTS, D), F32),
        compiler_params=pltpu.CompilerParams(dimension_semantics=("arbitrary",),
                                             vmem_limit_bytes=VMEM_LIMIT),
        name="moe_scatter",
    )(pend, dest3, h2.reshape(T // SUBLANES, SUBLANES, D))


def _expert_kernel(te_ref, nu_ref, xs_ref, wg_ref, wu_ref, wd_ref, ys_ref, wgb, wub, wdb):
    i = pl.program_id(0)
    changed = (i == 0) | (te_ref[i] != te_ref[jnp.maximum(i - 1, 0)])

    @pl.when(changed)
    def _():
        wgb[...] = wg_ref[...].astype(BF16)
        wub[...] = wu_ref[...].astype(BF16)
        wdb[...] = wd_ref[...].astype(BF16)

    @pl.when(i < nu_ref[0])
    def _():
        x = xs_ref[...].astype(BF16)
        a = jnp.dot(x, wgb[...], preferred_element_type=F32)
        u = jnp.dot(x, wub[...], preferred_element_type=F32)
        hid = (jax.nn.silu(a) * u).astype(BF16)
        ys_ref[...] = jnp.dot(hid, wdb[...], preferred_element_type=F32)

    @pl.when(i >= nu_ref[0])
    def _():
        ys_ref[...] = jnp.zeros_like(ys_ref)


def _experts(layer, tile_expert, n_used, xs, w_gate, w_up, w_down):
    wmap = lambda i, te, nu: (layer, te[i], 0, 0)
    tile = lambda i, te, nu: (i, 0)
    used_tile = lambda i, te, nu: (jnp.minimum(i, nu[0] - 1), 0)
    return pl.pallas_call(
        _expert_kernel,
        grid_spec=pltpu.PrefetchScalarGridSpec(
            num_scalar_prefetch=2, grid=(N_TILES,),
            in_specs=[pl.BlockSpec((TE, D), used_tile),
                      pl.BlockSpec((None, None, D, D_EXPERT), wmap),
                      pl.BlockSpec((None, None, D, D_EXPERT), wmap),
                      pl.BlockSpec((None, None, D_EXPERT, D), wmap)],
            out_specs=pl.BlockSpec((TE, D), tile),
            scratch_shapes=[pltpu.VMEM((D, D_EXPERT), BF16), pltpu.VMEM((D, D_EXPERT), BF16),
                            pltpu.VMEM((D_EXPERT, D), BF16)]),
        out_shape=jax.ShapeDtypeStruct((N_SLOTS, D), F32),
        compiler_params=pltpu.CompilerParams(dimension_semantics=("arbitrary",),
                                             vmem_limit_bytes=VMEM_LIMIT),
        name="moe_experts",
    )(tile_expert, n_used, xs, w_gate, w_up, w_down)


def _final_kernel(xp_ref, y2_ref, ri_ref, g_ref, o_ref):
    o_ref[...] = _rms(_combine(xp_ref[...], y2_ref[...], ri_ref[...]), g_ref[...])


def _final(xp, y2, ri, g):
    tile = lambda i: (i, 0)
    return pl.pallas_call(
        _final_kernel,
        grid=(T // TM,),
        in_specs=[pl.BlockSpec((TM, D), tile), pl.BlockSpec((TM, 2 * D), tile),
                  pl.BlockSpec((TM, LANES), tile), _const_spec((1, D))],
        out_specs=pl.BlockSpec((TM, D), tile),
        out_shape=jax.ShapeDtypeStruct((T, D), F32),
        compiler_params=pltpu.CompilerParams(dimension_semantics=("arbitrary",)),
        name="final_norm",
    )(xp, y2, ri, g)


def _plan(ri, cnt):
    eid = ri[:, 2:4].astype(I32)
    rank = ri[:, 4:6].astype(I32)
    counts = cnt[0, EXPERT_LANE0:EXPERT_LANE0 + N_EXPERTS].astype(I32)
    padded = ((counts + TE - 1) // TE) * TE
    pend = jnp.cumsum(padded)
    pstart = pend - padded
    sel = eid[:, :, None] == jnp.arange(N_EXPERTS, dtype=I32)
    dest = jnp.sum(jnp.where(sel, pstart, 0), axis=-1) + rank
    tile_start = jnp.arange(N_TILES, dtype=I32) * TE
    tile_expert = jnp.sum((tile_start[:, None] >= pend[None, :]).astype(I32), axis=-1)
    tile_expert = jnp.minimum(tile_expert, N_EXPERTS - 1)
    n_used = (pend[-1:] // TE).astype(I32)
    return dest.reshape(-1), tile_expert, n_used, pend.astype(I32)


def kernel(x, norm_mix_g, norm_ffn_g, a_w_in, a_b_in, a_norm_v, a_w_spatial, a_b_spatial, a_w_out, b_w_in, b_w_group, b_scale, b_w_out, router_group_w, router_group_b, router_expert_w, router_expert_b, w_gate, w_up, w_down, final_norm_g):
    xp = x.reshape(T, D)
    moe_in = None
    row = lambda a: a.reshape(1, -1)
    for i in range(DEPTH):
        j = i // 2
        wr = jnp.concatenate([router_group_w[i], router_expert_w[i]], axis=1)
        wr = jnp.pad(wr, ((0, 0), (0, LANES - wr.shape[1]))).astype(BF16)
        br = jnp.concatenate([router_group_b[i], router_expert_b[i]])
        br = jnp.pad(br, (0, LANES - br.shape[0])).reshape(1, LANES)
        gmix, gffn = row(norm_mix_g[i]), row(norm_ffn_g[i])
        if i % 2 == 0:
            xp, h2, ri, cnt = _sgu_layer(
                xp, moe_in, gmix, a_w_in[j].astype(BF16), row(a_b_in[j]), row(a_norm_v[j]),
                a_w_spatial[j].astype(BF16), a_b_spatial[j].T, a_w_out[j].astype(BF16),
                gffn, wr, br)
        else:
            xp, h2, ri, cnt = _pool_layer(
                xp, moe_in, gmix, b_w_in[j].astype(BF16), b_w_group[j].astype(BF16),
                row(b_scale[j]), b_w_out[j].astype(BF16), gffn, wr, br)
        dest, tile_expert, n_used, pend = _plan(ri, cnt)
        xs = _dispatch(pend, dest, h2)
        ys = _experts(i, tile_expert, n_used, xs, w_gate, w_up, w_down)
        y2 = _undispatch(dest, ys)
        moe_in = (y2, ri)
    out = _final(xp, moe_in[0], moe_in[1], row(final_norm_g))
    return out.reshape(BATCH, SEQ, D)
```

```python
import functools

import jax
import jax.numpy as jnp
from jax import lax
from jax.experimental import pallas as pl
from jax.experimental.pallas import tpu as pltpu

F32 = jnp.float32
BF16 = jnp.bfloat16
I32 = jnp.int32

D = 1024
BATCH = 16
SEQ = 2048
T = BATCH * SEQ
DEPTH = 4
CHUNK = 128
HEADS = 8
SGU_W = 2 * D
HEAD_DIM = SGU_W // HEADS
POOL_WINDOWS = (2, 4, 8, 16)
GROUP_DIM = D // len(POOL_WINDOWS)
N_GROUPS = 4
EPG = 8
N_EXPERTS = N_GROUPS * EPG
TOP_K = 2
D_EXPERT = D // 2
EPS = 1e-6

LANES = 128
SUBLANES = 8
TM = 512
TE = 256
N_TILES = (T * TOP_K) // TE + N_EXPERTS
N_SLOTS = N_TILES * TE
TD = 1024
HALO = SUBLANES
EXPERT_LANE0 = N_GROUPS
N_COL = 512
VMEM_LIMIT = 56 * 1024 * 1024


def _rms(x, g):
    return x * lax.rsqrt(jnp.mean(x * x, axis=-1, keepdims=True) + EPS) * g


def _combine(xp, y2, ri):
    return xp + (ri[:, 0:1] * y2[:, :D] + ri[:, 1:2] * y2[:, D:])


def _route(xn, gffn_ref, wr_ref, br_ref, run_ref, h2_ref, ri_ref, cnt_ref):
    tm = xn.shape[0]
    h2 = _rms(xn, gffn_ref[...])
    h2_ref[...] = h2
    logits = jnp.dot(h2.astype(BF16), wr_ref[...], preferred_element_type=F32) + br_ref[...]
    lane = lax.broadcasted_iota(I32, (tm, LANES), 1).astype(F32)
    big = float(LANES)

    def first_lane(mask):
        return jnp.min(jnp.where(mask, lane, big), axis=-1, keepdims=True)

    gmask = lane < float(N_GROUPS)
    gmax = jnp.max(jnp.where(gmask, logits, -jnp.inf), axis=-1, keepdims=True)
    gexp = jnp.where(gmask, jnp.exp(logits - gmax), 0.0)
    gprob = gexp / jnp.sum(gexp, axis=-1, keepdims=True)
    g_w = jnp.max(gprob, axis=-1, keepdims=True)
    g_idx = first_lane(gmask & (gprob == g_w))
    lo = float(EXPERT_LANE0) + float(EPG) * g_idx
    emask = (lane >= lo) & (lane < lo + float(EPG))
    emax = jnp.max(jnp.where(emask, logits, -jnp.inf), axis=-1, keepdims=True)
    eexp = jnp.where(emask, jnp.exp(logits - emax), 0.0)
    eprob = jnp.where(emask, eexp / jnp.sum(eexp, axis=-1, keepdims=True), -1.0)
    p1 = jnp.max(eprob, axis=-1, keepdims=True)
    i1 = first_lane(eprob == p1)
    eprob2 = jnp.where(lane == i1, -1.0, eprob)
    p2 = jnp.max(eprob2, axis=-1, keepdims=True)
    i2 = first_lane(eprob2 == p2)
    den = p1 + p2
    gate1 = g_w * (p1 / den)
    gate2 = g_w * (p2 / den)
    oh1 = lane == i1
    oh2 = lane == i2
    row = lax.broadcasted_iota(I32, (tm, tm), 0)
    col = lax.broadcasted_iota(I32, (tm, tm), 1)
    ltri = (col < row).astype(BF16)
    c1 = jnp.dot(ltri, oh1.astype(BF16), preferred_element_type=F32)
    c2 = jnp.dot(ltri, oh2.astype(BF16), preferred_element_type=F32)
    tot1 = jnp.sum(oh1.astype(F32), axis=0, keepdims=True)
    tot2 = jnp.sum(oh2.astype(F32), axis=0, keepdims=True)
    run = run_ref[...]
    r1 = jnp.sum(jnp.where(oh1, run + c1, 0.0), axis=-1, keepdims=True)
    r2 = jnp.sum(jnp.where(oh2, run + tot1 + c2, 0.0), axis=-1, keepdims=True)
    run = run + tot1 + tot2
    run_ref[...] = run
    cnt_ref[...] = run
    e0 = float(EXPERT_LANE0)
    vals = (gate1, gate2, i1 - e0, i2 - e0, r1, r2)
    ri = jnp.zeros((tm, LANES), F32)
    for k, v in enumerate(vals):
        ri = jnp.where(lane == float(k), v, ri)
    ri_ref[...] = ri


def _gelu_exact(z):
    return 0.5 * z * (1.0 + lax.erf(z * (0.5 ** 0.5)))


def _sgu_kernel(has_moe, *refs):
    if has_moe:
        xp_ref, y2_ref, rip_ref = refs[:3]
        refs = refs[3:]
    else:
        xp_ref = refs[0]
        refs = refs[1:]
    (gmix_ref, win_ref, bin_ref, gv_ref, ws_ref, bst_ref, wout_ref, gffn_ref, wr_ref, br_ref,
     xn_ref, h2_ref, ri_ref, cnt_ref, u_ref, v_ref, y_ref, run_ref) = refs

    @pl.when(pl.program_id(0) == 0)
    def _():
        run_ref[...] = jnp.zeros_like(run_ref)

    x = xp_ref[...]
    if has_moe:
        x = _combine(x, y2_ref[...], rip_ref[...])
    h = _rms(x, gmix_ref[...]).astype(BF16)
    ss = jnp.zeros((TM, 1), F32)
    for j in range(0, SGU_W, N_COL):
        z = jnp.dot(h, win_ref[:, j:j + N_COL], preferred_element_type=F32) + bin_ref[:, j:j + N_COL]
        u_ref[:, j:j + N_COL] = _gelu_exact(z)
    for j in range(0, SGU_W, N_COL):
        jj = SGU_W + j
        z = jnp.dot(h, win_ref[:, jj:jj + N_COL], preferred_element_type=F32) + bin_ref[:, jj:jj + N_COL]
        vv = _gelu_exact(z)
        v_ref[:, j:j + N_COL] = vv
        ss = ss + jnp.sum(vv * vv, axis=-1, keepdims=True)
    rinv = lax.rsqrt(ss * (1.0 / SGU_W) + EPS)
    for c in range(TM // CHUNK):
        rows = slice(c * CHUNK, (c + 1) * CHUNK)
        for hd in range(HEADS):
            cols = slice(hd * HEAD_DIM, (hd + 1) * HEAD_DIM)
            vn = (v_ref[rows, cols] * rinv[rows] * gv_ref[:, cols]).astype(BF16)
            sv = jnp.dot(ws_ref[hd], vn, preferred_element_type=F32) + bst_ref[:, hd:hd + 1]
            y_ref[rows, cols] = (u_ref[rows, cols] * sv).astype(BF16)
    xn = x + jnp.dot(y_ref[...], wout_ref[...], preferred_element_type=F32)
    xn_ref[...] = xn
    _route(xn, gffn_ref, wr_ref, br_ref, run_ref, h2_ref, ri_ref, cnt_ref)


def _pool_kernel(has_moe, *refs):
    n_in = 9 if has_moe else 3
    xin = refs[:n_in]
    (gmix_ref, win_ref, wg_ref, sc_ref, wout_ref, gffn_ref, wr_ref, br_ref,
     xn_ref, h2_ref, ri_ref, cnt_ref, y_ref, run_ref) = refs[n_in:]
    b = pl.program_id(0)
    i = pl.program_id(1)
    n_i = pl.num_programs(1)

    @pl.when((b == 0) & (i == 0))
    def _():
        run_ref[...] = jnp.zeros_like(run_ref)

    def load(k):
        if has_moe:
            return _combine(xin[k][...], xin[3 + k][...], xin[6 + k][...])
        return xin[k][...]

    x = load(0)
    g = gmix_ref[...]
    h = _rms(x, g).astype(BF16)
    xh = jnp.concatenate([load(1), load(2)], axis=0)
    hh = _rms(xh, g).astype(BF16)
    v = jnp.dot(h, win_ref[...], preferred_element_type=F32)
    vh = jnp.dot(hh, win_ref[...], preferred_element_type=F32)
    vb = v.astype(BF16)
    vhb = jnp.concatenate([vh, jnp.zeros((LANES - 2 * HALO, D), F32)], axis=0).astype(BF16)

    r = lax.broadcasted_iota(I32, (TM, TM), 0)
    m = lax.broadcasted_iota(I32, (TM, TM), 1)
    rh = lax.broadcasted_iota(I32, (TM, LANES), 0)
    ch = lax.broadcasted_iota(I32, (TM, LANES), 1)
    jh = jnp.where(ch < HALO, ch, ch + TM)
    okh = ((ch < HALO) & (i > 0)) | ((ch >= HALO) & (ch < 2 * HALO) & (i < n_i - 1))
    pos = (i * TM + lax.broadcasted_iota(I32, (TM, 1), 0))
    for gi, w in enumerate(POOL_WINDOWS):
        cols = slice(gi * GROUP_DIM, (gi + 1) * GROUP_DIM)
        half = w // 2
        pm = ((m >= r - half) & (m <= r + half - 1)).astype(BF16)
        ph = (okh & (jh >= rh + HALO - half) & (jh <= rh + HALO + half - 1)).astype(BF16)
        s = (jnp.dot(pm, vb[:, cols], preferred_element_type=F32)
             + jnp.dot(ph, vhb[:, cols], preferred_element_type=F32))
        lo = jnp.clip(pos - half, 0, SEQ)
        hi = jnp.clip(pos - half + w, 0, SEQ)
        cnt = (hi - lo).astype(F32)
        dlt = (s / cnt - v[:, cols]).astype(BF16)
        yg = jnp.dot(dlt, wg_ref[gi], preferred_element_type=F32)
        y_ref[:, cols] = (yg * sc_ref[:, cols]).astype(BF16)
    xn = x + jnp.dot(y_ref[...], wout_ref[...], preferred_element_type=F32)
    xn_ref[...] = xn
    _route(xn, gffn_ref, wr_ref, br_ref, run_ref, h2_ref, ri_ref, cnt_ref)


def _const_spec(shape):
    nd = len(shape)
    return pl.BlockSpec(shape, lambda *_: (0,) * nd)


def _mixer_out(n_steps_map):
    out_shape = (jax.ShapeDtypeStruct((T, D), F32), jax.ShapeDtypeStruct((T, D), F32),
                 jax.ShapeDtypeStruct((T, LANES), F32), jax.ShapeDtypeStruct((1, LANES), F32))
    out_specs = (pl.BlockSpec((TM, D), n_steps_map), pl.BlockSpec((TM, D), n_steps_map),
                 pl.BlockSpec((TM, LANES), n_steps_map), _const_spec((1, LANES)))
    return out_shape, out_specs


def _sgu_layer(xp, moe_in, gmix, win, b_in, gv, ws, bst, wout, gffn, wr, br):
    has_moe = moe_in is not None
    tile = lambda i: (i, 0)
    ins, specs = [xp], [pl.BlockSpec((TM, D), tile)]
    if has_moe:
        ins += list(moe_in)
        specs += [pl.BlockSpec((TM, 2 * D), tile), pl.BlockSpec((TM, LANES), tile)]
    weights = [gmix, win, b_in, gv, ws, bst, wout, gffn, wr, br]
    ins += weights
    specs += [_const_spec(w.shape) for w in weights]
    out_shape, out_specs = _mixer_out(tile)
    return pl.pallas_call(
        functools.partial(_sgu_kernel, has_moe),
        grid=(T // TM,),
        in_specs=specs,
        out_specs=out_specs,
        out_shape=out_shape,
        scratch_shapes=[pltpu.VMEM((TM, SGU_W), F32), pltpu.VMEM((TM, SGU_W), F32),
                        pltpu.VMEM((TM, SGU_W), BF16), pltpu.VMEM((1, LANES), F32)],
        compiler_params=pltpu.CompilerParams(dimension_semantics=("arbitrary",),
                                             vmem_limit_bytes=VMEM_LIMIT),
        name="sgu_mixer",
    )(*ins)


def _pool_layer(xp, moe_in, gmix, win, wg, sc, wout, gffn, wr, br):
    has_moe = moe_in is not None
    n_i = SEQ // TM
    tile = lambda b, i: (b * n_i + i, 0)
    per8 = TM // HALO
    prev = lambda b, i: (jnp.maximum(b * (SEQ // HALO) + i * per8 - 1, 0), 0)
    nxt = lambda b, i: (jnp.minimum(b * (SEQ // HALO) + (i + 1) * per8, T // HALO - 1), 0)
    arrays = [xp] + (list(moe_in) if has_moe else [])
    widths = [D] + ([2 * D, LANES] if has_moe else [])
    ins, specs = [], []
    for a, wdt in zip(arrays, widths):
        ins += [a, a, a]
        specs += [pl.BlockSpec((TM, wdt), tile), pl.BlockSpec((HALO, wdt), prev),
                  pl.BlockSpec((HALO, wdt), nxt)]
    weights = [gmix, win, wg, sc, wout, gffn, wr, br]
    ins += weights
    specs += [_const_spec(w.shape) for w in weights]
    out_shape, out_specs = _mixer_out(tile)
    return pl.pallas_call(
        functools.partial(_pool_kernel, has_moe),
        grid=(BATCH, n_i),
        in_specs=specs,
        out_specs=out_specs,
        out_shape=out_shape,
        scratch_shapes=[pltpu.VMEM((TM, D), BF16), pltpu.VMEM((1, LANES), F32)],
        compiler_params=pltpu.CompilerParams(dimension_semantics=("arbitrary", "arbitrary"),
                                             vmem_limit_bytes=VMEM_LIMIT),
        name="pool_mixer",
    )(*ins)


def _gather_row(ys_ref, y2_ref, sem, s_row, q, s, k):
    return pltpu.make_async_copy(ys_ref.at[pl.ds(s_row, 1)],
                                 y2_ref.at[q, pl.ds(s, 1), pl.ds(k * D, D)], sem)


def _gather_kernel(dest_ref, ys_ref, y2_ref, sem):
    def issue(q, c):
        for s in range(SUBLANES):
            for k in range(TOP_K):
                idx = dest_ref[0, q * (TOP_K * SUBLANES) + (TOP_K * s + k)]
                _gather_row(ys_ref, y2_ref, sem, idx, q, s, k).start(priority=k)
        return c

    def drain(q, c):
        for s in range(SUBLANES):
            for k in range(TOP_K):
                _gather_row(ys_ref, y2_ref, sem, 0, 0, s, k).wait()
        return c

    lax.fori_loop(0, TD // SUBLANES, issue, 0)
    lax.fori_loop(0, TD // SUBLANES, drain, 0)


def _dest_blocks(dest):
    spec = pl.BlockSpec((None, 1, TOP_K * TD), lambda i, *_: (i, 0, 0), memory_space=pltpu.SMEM)
    return dest.reshape(T // TD, 1, TOP_K * TD), spec


def _undispatch(dest, ys):
    dest3, dest_spec = _dest_blocks(dest)
    return pl.pallas_call(
        _gather_kernel,
        grid=(T // TD,),
        in_specs=[dest_spec, pl.BlockSpec(memory_space=pl.ANY)],
        out_specs=pl.BlockSpec((TD // SUBLANES, SUBLANES, TOP_K * D), lambda i: (i, 0, 0)),
        scratch_shapes=[pltpu.SemaphoreType.DMA(())],
        out_shape=jax.ShapeDtypeStruct((T // SUBLANES, SUBLANES, TOP_K * D), F32),
        compiler_params=pltpu.CompilerParams(dimension_semantics=("arbitrary",),
                                             vmem_limit_bytes=VMEM_LIMIT),
        name="moe_gather",
    )(dest3, ys).reshape(T, TOP_K * D)


def _scatter_row(h2_ref, xs_ref, sem, q, s, d_row):
    return pltpu.make_async_copy(h2_ref.at[q, pl.ds(s, 1)], xs_ref.at[pl.ds(d_row, 1)], sem)


def _zero_tile(zbuf, xs_ref, sem, end_row):
    start = pl.multiple_of(end_row - TE, TE)
    return pltpu.make_async_copy(zbuf, xs_ref.at[pl.ds(start, TE)], sem)


def _scatter_kernel(pend_ref, dest_ref, h2_ref, xs_ref, zbuf, sem, zsem):
    @pl.when(pl.program_id(0) == 0)
    def _():
        zbuf[...] = jnp.zeros_like(zbuf)
        used_rows = pend_ref[N_EXPERTS - 1]
        for wait in (False, True):
            for e in range(N_EXPERTS):
                first = pend_ref[e - 1] if e else 0

                @pl.when(pend_ref[e] > first)
                def _():
                    cp = _zero_tile(zbuf, xs_ref, zsem, pend_ref[e])
                    cp.wait() if wait else cp.start()

            for j in range(1, N_TILES - (T * TOP_K) // TE + 1):
                end_row = used_rows + j * TE

                @pl.when(end_row <= N_SLOTS)
                def _():
                    cp = _zero_tile(zbuf, xs_ref, zsem, end_row)
                    cp.wait() if wait else cp.start()

    def issue(q, c):
        for s in range(SUBLANES):
            for k in range(TOP_K):
                idx = dest_ref[0, q * (TOP_K * SUBLANES) + (TOP_K * s + k)]
                _scatter_row(h2_ref, xs_ref, sem, q, s, idx).start(priority=k)
        return c

    def drain(q, c):
        for s in range(SUBLANES):
            for _ in range(TOP_K):
                _scatter_row(h2_ref, xs_ref, sem, 0, s, 0).wait()
        return c

    lax.fori_loop(0, TD // SUBLANES, issue, 0)
    lax.fori_loop(0, TD // SUBLANES, drain, 0)


def _dispatch(pend, dest, h2):
    dest3, dest_spec = _dest_blocks(dest)
    return pl.pallas_call(
        _scatter_kernel,
        grid_spec=pltpu.PrefetchScalarGridSpec(
            num_scalar_prefetch=1, grid=(T // TD,),
            in_specs=[dest_spec,
                      pl.BlockSpec((TD // SUBLANES, SUBLANES, D), lambda i, p: (i, 0, 0))],
            out_specs=pl.BlockSpec(memory_space=pl.ANY),
            scratch_shapes=[pltpu.VMEM((TE, D), F32), pltpu.SemaphoreType.DMA(()),
                            pltpu.SemaphoreType.DMA(())]),
        out_shape=jax.ShapeDtypeStruct((N_SLOTS, D), F32),
        compiler_params=pltpu.CompilerParams(dimension_semantics=("arbitrary",),
                                             vmem_limit_bytes=VMEM_LIMIT),
        name="moe_scatter",
    )(pend, dest3, h2.reshape(T // SUBLANES, SUBLANES, D))


def _expert_kernel(te_ref, nu_ref, xs_ref, wg_ref, wu_ref, wd_ref, ys_ref, wgb, wub, wdb):
    i = pl.program_id(0)
    changed = (i == 0) | (te_ref[i] != te_ref[jnp.maximum(i - 1, 0)])

    @pl.when(changed)
    def _():
        wgb[...] = wg_ref[...].astype(BF16)
        wub[...] = wu_ref[...].astype(BF16)
        wdb[...] = wd_ref[...].astype(BF16)

    @pl.when(i < nu_ref[0])
    def _():
        x = xs_ref[...].astype(BF16)
        a = jnp.dot(x, wgb[...], preferred_element_type=F32)
        u = jnp.dot(x, wub[...], preferred_element_type=F32)
        hid = (jax.nn.silu(a) * u).astype(BF16)
        ys_ref[...] = jnp.dot(hid, wdb[...], preferred_element_type=F32)

    @pl.when(i >= nu_ref[0])
    def _():
        ys_ref[...] = jnp.zeros_like(ys_ref)


def _experts(layer, tile_expert, n_used, xs, w_gate, w_up, w_down):
    wmap = lambda i, te, nu: (layer, te[i], 0, 0)
    tile = lambda i, te, nu: (i, 0)
    used_tile = lambda i, te, nu: (jnp.minimum(i, nu[0] - 1), 0)
    return pl.pallas_call(
        _expert_kernel,
        grid_spec=pltpu.PrefetchScalarGridSpec(
            num_scalar_prefetch=2, grid=(N_TILES,),
            in_specs=[pl.BlockSpec((TE, D), used_tile),
                      pl.BlockSpec((None, None, D, D_EXPERT), wmap),
                      pl.BlockSpec((None, None, D, D_EXPERT), wmap),
                      pl.BlockSpec((None, None, D_EXPERT, D), wmap)],
            out_specs=pl.BlockSpec((TE, D), tile),
            scratch_shapes=[pltpu.VMEM((D, D_EXPERT), BF16), pltpu.VMEM((D, D_EXPERT), BF16),
                            pltpu.VMEM((D_EXPERT, D), BF16)]),
        out_shape=jax.ShapeDtypeStruct((N_SLOTS, D), F32),
        compiler_params=pltpu.CompilerParams(dimension_semantics=("arbitrary",),
                                             vmem_limit_bytes=VMEM_LIMIT),
        name="moe_experts",
    )(tile_expert, n_used, xs, w_gate, w_up, w_down)


def _final_kernel(xp_ref, y2_ref, ri_ref, g_ref, o_ref):
    o_ref[...] = _rms(_combine(xp_ref[...], y2_ref[...], ri_ref[...]), g_ref[...])


def _final(xp, y2, ri, g):
    tile = lambda i: (i, 0)
    return pl.pallas_call(
        _final_kernel,
        grid=(T // TM,),
        in_specs=[pl.BlockSpec((TM, D), tile), pl.BlockSpec((TM, 2 * D), tile),
                  pl.BlockSpec((TM, LANES), tile), _const_spec((1, D))],
        out_specs=pl.BlockSpec((TM, D), tile),
        out_shape=jax.ShapeDtypeStruct((T, D), F32),
        compiler_params=pltpu.CompilerParams(dimension_semantics=("arbitrary",)),
        name="final_norm",
    )(xp, y2, ri, g)


def _plan(ri, cnt):
    eid = ri[:, 2:4].astype(I32)
    rank = ri[:, 4:6].astype(I32)
    counts = cnt[0, EXPERT_LANE0:EXPERT_LANE0 + N_EXPERTS].astype(I32)
    padded = ((counts + TE - 1) // TE) * TE
    pend = jnp.cumsum(padded)
    pstart = pend - padded
    sel = eid[:, :, None] == jnp.arange(N_EXPERTS, dtype=I32)
    dest = jnp.sum(jnp.where(sel, pstart, 0), axis=-1) + rank
    tile_start = jnp.arange(N_TILES, dtype=I32) * TE
    tile_expert = jnp.sum((tile_start[:, None] >= pend[None, :]).astype(I32), axis=-1)
    tile_expert = jnp.minimum(tile_expert, N_EXPERTS - 1)
    n_used = (pend[-1:] // TE).astype(I32)
    return dest.reshape(-1), tile_expert, n_used, pend.astype(I32)


def kernel(x, norm_mix_g, norm_ffn_g, a_w_in, a_b_in, a_norm_v, a_w_spatial, a_b_spatial, a_w_out, b_w_in, b_w_group, b_scale, b_w_out, router_group_w, router_group_b, router_expert_w, router_expert_b, w_gate, w_up, w_down, final_norm_g):
    xp = x.reshape(T, D)
    moe_in = None
    row = lambda a: a.reshape(1, -1)
    for i in range(DEPTH):
        j = i // 2
        wr = jnp.concatenate([router_group_w[i], router_expert_w[i]], axis=1)
        wr = jnp.pad(wr, ((0, 0), (0, LANES - wr.shape[1]))).astype(BF16)
        br = jnp.concatenate([router_group_b[i], router_expert_b[i]])
        br = jnp.pad(br, (0, LANES - br.shape[0])).reshape(1, LANES)
        gmix, gffn = row(norm_mix_g[i]), row(norm_ffn_g[i])
        if i % 2 == 0:
            xp, h2, ri, cnt = _sgu_layer(
                xp, moe_in, gmix, a_w_in[j].astype(BF16), row(a_b_in[j]), row(a_norm_v[j]),
                a_w_spatial[j].astype(BF16), a_b_spatial[j].T, a_w_out[j].astype(BF16),
                gffn, wr, br)
        else:
            xp, h2, ri, cnt = _pool_layer(
                xp, moe_in, gmix, b_w_in[j].astype(BF16), b_w_group[j].astype(BF16),
                row(b_scale[j]), b_w_out[j].astype(BF16), gffn, wr, br)
        dest, tile_expert, n_used, pend = _plan(ri, cnt)
        xs = _dispatch(pend, dest, h2)
        ys = _experts(i, tile_expert, n_used, xs, w_gate, w_up, w_down)
        y2 = _undispatch(dest, ys)
        moe_in = (y2, ri)
    out = _final(xp, moe_in[0], moe_in[1], row(final_norm_g))
    return out.reshape(BATCH, SEQ, D)
```
